```python
import jax, jax.numpy as jnp
from jax import lax
import numpy as np

D_MODEL = 2048
BATCH = 8
SEQ = 2048
DEPTH = 2

HEAD_DIM = 128
ATTN_WIDTH = D_MODEL // 2
CONV_DIM = D_MODEL - ATTN_WIDTH
MIX_DIM = ATTN_WIDTH + CONV_DIM
N_Q_HEADS = ATTN_WIDTH // HEAD_DIM
N_KV_HEADS = max(1, N_Q_HEADS // 4)
Q_PER_KV = N_Q_HEADS // N_KV_HEADS
Q_DIM = N_Q_HEADS * HEAD_DIM
KV_DIM = N_KV_HEADS * HEAD_DIM
N_CONV_GROUPS = CONV_DIM // HEAD_DIM
CONV_WIDTH = 3
IN_PROJ_DIM = Q_DIM + 2 * KV_DIM + 3 * CONV_DIM
DILATED_BRANCHES = ((128, 1), (512, 4), (2048, 16))
D_FF = 5632
FFN_RESIDUAL_WEIGHT = 0.5
NORM_EPS = 1e-6

kernel_name = 'hybrid_dilated_attn_shortconv_macaron'


def rms_norm(x, gain):
    xf = x.astype(jnp.float32)
    y = xf * lax.rsqrt(jnp.mean(xf * xf, axis=-1, keepdims=True) + NORM_EPS)
    return (y * gain.astype(jnp.float32)).astype(x.dtype)


def swiglu(x, w_gate_up, w_down):
    gu = x @ w_gate_up
    gate, up = jnp.split(gu, 2, axis=-1)
    return (jax.nn.silu(gate) * up) @ w_down


def dilated_branch(q, k, v, window, dilation):
    b, s, g, r, hd = q.shape
    span = window // dilation
    blk = span
    sub_len = -(-s // dilation)
    nb = -(-sub_len // blk)
    sp = nb * blk * dilation
    pad = sp - s
    qb = jnp.pad(q, ((0, 0), (0, pad), (0, 0), (0, 0), (0, 0))).reshape(b, nb, blk, dilation, g, r, hd)
    kb = jnp.pad(k, ((0, 0), (0, pad), (0, 0), (0, 0))).reshape(b, nb, blk, dilation, g, hd)
    vb = jnp.pad(v, ((0, 0), (0, pad), (0, 0), (0, 0))).reshape(b, nb, blk, dilation, g, hd)

    def with_prev(t):
        prev = jnp.pad(t[:, :-1], ((0, 0), (1, 0), (0, 0), (0, 0), (0, 0), (0, 0)))
        return jnp.concatenate([prev, t], axis=2)

    kw = with_prev(kb)
    vw = with_prev(vb)
    scores = jnp.einsum('bnqeghd,bnkegd->bneghqk', qb, kw,
                        preferred_element_type=jnp.float32) * (hd ** -0.5)
    qi = jnp.arange(blk)[:, None]
    kj = jnp.arange(2 * blk)[None, :]
    dist = qi - kj + blk
    band = (dist >= 0) & (dist <= span)
    valid = band[None] & ((jnp.arange(nb)[:, None, None] > 0) | (kj[None] >= blk))
    scores = jnp.where(valid[None, :, None, None, None], scores, -jnp.inf)
    m = jnp.max(scores, axis=-1, keepdims=True)
    p = jnp.exp(scores - m)
    den = jnp.sum(p, axis=-1, keepdims=True)
    o = jnp.einsum('bneghqk,bnkegd->bnqeghd', p / den, vw.astype(jnp.float32))
    lse = jnp.moveaxis((m + jnp.log(den))[..., 0], -1, 2)
    o = o.reshape(b, sp, g, r, hd)[:, :s]
    lse = lse.reshape(b, sp, g, r)[:, :s]
    return o, lse


def dilated_attention(q, k, v):
    b, s, _ = q.shape
    qh = q.reshape(b, s, N_KV_HEADS, Q_PER_KV, HEAD_DIM)
    kh = k.reshape(b, s, N_KV_HEADS, HEAD_DIM)
    vh = v.reshape(b, s, N_KV_HEADS, HEAD_DIM)
    outs, lses = [], []
    for window, dilation in DILATED_BRANCHES:
        o, lse = dilated_branch(qh, kh, vh, window, dilation)
        outs.append(o)
        lses.append(lse)
    weights = jax.nn.softmax(jnp.stack(lses, axis=0), axis=0)
    o = jnp.sum(weights[..., None] * jnp.stack(outs, axis=0), axis=0)
    return o.reshape(b, s, Q_DIM).astype(q.dtype)


def gated_short_conv(h, b_gate, c_gate, conv_w):
    u = c_gate * h
    y = lax.conv_general_dilated(
        u, conv_w[:, None, :].astype(u.dtype), window_strides=(1,),
        padding=[(CONV_WIDTH - 1, 0)], dimension_numbers=('NWC', 'WIO', 'NWC'),
        feature_group_count=u.shape[-1])
    return b_gate * y


def setup_inputs(seed: int = 0) -> dict:
    key = jax.random.key(seed)
    ks = jax.random.split(key, 20)

    def w(k, shape, fan_in):
        return jax.random.normal(k, shape, jnp.float32) * (fan_in ** -0.5)

    def gain(k, shape):
        return 1.0 + 0.02 * jax.random.normal(k, shape, jnp.float32)

    return {
        'x': jax.random.normal(ks[0], (BATCH, SEQ, D_MODEL), jnp.float32),
        'ffn1_norm_pre': gain(ks[1], (DEPTH, D_MODEL)),
        'ffn1_w_gate_up': w(ks[2], (DEPTH, D_MODEL, 2 * D_FF), D_MODEL),
        'ffn1_w_down': w(ks[3], (DEPTH, D_FF, D_MODEL), D_FF),
        'ffn1_norm_post': gain(ks[4], (DEPTH, D_MODEL)),
        'mix_norm_pre': gain(ks[5], (DEPTH, D_MODEL)),
        'w_in': w(ks[6], (DEPTH, D_MODEL, IN_PROJ_DIM), D_MODEL),
        'conv_w': w(ks[7], (DEPTH, CONV_WIDTH, CONV_DIM), CONV_WIDTH),
        'attn_out_norm': gain(ks[8], (DEPTH, Q_DIM)),
        'conv_out_norm': gain(ks[9], (DEPTH, CONV_DIM)),
        'w_out': w(ks[10], (DEPTH, MIX_DIM, D_MODEL), MIX_DIM),
        'mix_norm_post': gain(ks[11], (DEPTH, D_MODEL)),
        'ffn2_norm_pre': gain(ks[12], (DEPTH, D_MODEL)),
        'ffn2_w_gate_up': w(ks[13], (DEPTH, D_MODEL, 2 * D_FF), D_MODEL),
        'ffn2_w_down': w(ks[14], (DEPTH, D_FF, D_MODEL), D_FF),
        'ffn2_norm_post': gain(ks[15], (DEPTH, D_MODEL)),
    }


def reference(x, ffn1_norm_pre, ffn1_w_gate_up, ffn1_w_down, ffn1_norm_post,
              mix_norm_pre, w_in, conv_w, attn_out_norm, conv_out_norm, w_out,
              mix_norm_post, ffn2_norm_pre, ffn2_w_gate_up, ffn2_w_down,
              ffn2_norm_post):
    split_at = [Q_DIM, Q_DIM + KV_DIM, Q_DIM + 2 * KV_DIM,
                Q_DIM + 2 * KV_DIM + CONV_DIM, Q_DIM + 2 * KV_DIM + 2 * CONV_DIM]
    for l in range(DEPTH):
        h = swiglu(rms_norm(x, ffn1_norm_pre[l]), ffn1_w_gate_up[l], ffn1_w_down[l])
        x = x + FFN_RESIDUAL_WEIGHT * rms_norm(h, ffn1_norm_post[l])
        h = rms_norm(x, mix_norm_pre[l])
        z = h @ w_in[l]
        q, k, v, hc, b_gate, c_gate = jnp.split(z, split_at, axis=-1)
        a = dilated_attention(q, k, v)
        c = gated_short_conv(hc, b_gate, c_gate, conv_w[l])
        mixed = jnp.concatenate([rms_norm(a, attn_out_norm[l]),
                                 rms_norm(c, conv_out_norm[l])], axis=-1) @ w_out[l]
        x = x + rms_norm(mixed, mix_norm_post[l])
        h = swiglu(rms_norm(x, ffn2_norm_pre[l]), ffn2_w_gate_up[l], ffn2_w_down[l])
        x = x + FFN_RESIDUAL_WEIGHT * rms_norm(h, ffn2_norm_post[l])
    return x
```

```python
import functools

import jax
import jax.numpy as jnp
from jax import lax
from jax.experimental import pallas as pl
from jax.experimental.pallas import tpu as pltpu

D_MODEL = 2048
HEAD_DIM = 128
ATTN_WIDTH = D_MODEL // 2
CONV_DIM = D_MODEL - ATTN_WIDTH
N_Q_HEADS = ATTN_WIDTH // HEAD_DIM
N_KV_HEADS = max(1, N_Q_HEADS // 4)
Q_PER_KV = N_Q_HEADS // N_KV_HEADS
Q_DIM = N_Q_HEADS * HEAD_DIM
KV_DIM = N_KV_HEADS * HEAD_DIM
CONV_WIDTH = 3
IN_PROJ_DIM = Q_DIM + 2 * KV_DIM + 3 * CONV_DIM
SPAN = 128
DILATIONS = (1, 4, 16)
FFN_RESIDUAL_WEIGHT = 0.5
NORM_EPS = 1e-6

V7X_VMEM_BYTES = 64 * 1024 * 1024
SUBLANES = 8

BF16 = jnp.bfloat16
F32 = jnp.float32

_OFF_K = Q_DIM
_OFF_V = Q_DIM + KV_DIM
_OFF_H = Q_DIM + 2 * KV_DIM
_OFF_B = _OFF_H + CONV_DIM
_OFF_C = _OFF_B + CONV_DIM


def _rms(x, gain):
    y = x * lax.rsqrt(jnp.mean(x * x, axis=-1, keepdims=True) + NORM_EPS)
    return y * gain


def _ffn_kernel(x_ref, gpre_ref, wg_ref, wu_ref, wd_ref, gpost_ref, o_ref, h_ref, acc_ref):
    j = pl.program_id(1)

    @pl.when(j == 0)
    def _():
        h_ref[...] = _rms(x_ref[...], gpre_ref[...]).astype(BF16)

    h = h_ref[...]
    g = jnp.dot(h, wg_ref[...], preferred_element_type=F32)
    u = jnp.dot(h, wu_ref[...], preferred_element_type=F32)
    a = (g * jax.nn.sigmoid(g) * u).astype(BF16)
    c = jnp.dot(a, wd_ref[...], preferred_element_type=F32)

    @pl.when(j == 0)
    def _():
        acc_ref[...] = c

    @pl.when(j > 0)
    def _():
        acc_ref[...] += c

    @pl.when(j == pl.num_programs(1) - 1)
    def _():
        o_ref[...] = x_ref[...] + FFN_RESIDUAL_WEIGHT * _rms(acc_ref[...], gpost_ref[...])


def _ffn(x, gpre, w_gu, w_down, gpost, layer, *, tm=512, tf=512):
    n, d = x.shape
    d_ff = w_down.shape[1]
    nj = d_ff // tf
    assert n % tm == 0 and d_ff % tf == 0
    vmem = (2 * 2 * tm * d * 4
            + tm * d * (2 + 4)
            + 2 * 3 * d * tf * 2
            + 4 * tm * tf * 4
            + tm * d * 4)
    return pl.pallas_call(
        _ffn_kernel,
        grid=(n // tm, nj),
        in_specs=[
            pl.BlockSpec((tm, d), lambda i, j: (i, 0)),
            pl.BlockSpec((None, 1, d), lambda i, j: (layer, 0, 0)),
            pl.BlockSpec((None, d, tf), lambda i, j: (layer, 0, j)),
            pl.BlockSpec((None, d, tf), lambda i, j: (layer, 0, j + nj)),
            pl.BlockSpec((None, tf, d), lambda i, j: (layer, j, 0)),
            pl.BlockSpec((None, 1, d), lambda i, j: (layer, 0, 0)),
        ],
        out_specs=pl.BlockSpec((tm, d), lambda i, j: (i, 0)),
        out_shape=jax.ShapeDtypeStruct((n, d), F32),
        scratch_shapes=[pltpu.VMEM((tm, d), BF16), pltpu.VMEM((tm, d), F32)],
        compiler_params=pltpu.CompilerParams(
            dimension_semantics=("parallel", "arbitrary"),
            vmem_limit_bytes=min(vmem + (4 << 20), V7X_VMEM_BYTES - (4 << 20))),
        name="ffn",
    )(x, gpre, w_gu, w_gu, w_down, gpost)


def _in_proj_kernel(x_ref, gpre_ref, w_ref, cw_ref, gc_ref, q_ref, k_ref, v_ref, cn_ref,
                    u_ref, *, tiles_per_seq):
    i = pl.program_id(0)
    tm = x_ref.shape[0]
    h = _rms(x_ref[...], gpre_ref[...]).astype(BF16)

    def proj(off, width):
        return jnp.dot(h, w_ref[:, off:off + width], preferred_element_type=F32)

    zq = proj(0, Q_DIM)
    for hd in range(N_Q_HEADS):
        q_ref[hd] = zq[:, hd * HEAD_DIM:(hd + 1) * HEAD_DIM]
    zk = proj(_OFF_K, KV_DIM)
    zv = proj(_OFF_V, KV_DIM)
    for hd in range(N_KV_HEADS):
        k_ref[hd] = zk[:, hd * HEAD_DIM:(hd + 1) * HEAD_DIM]
        v_ref[hd] = zv[:, hd * HEAD_DIM:(hd + 1) * HEAD_DIM]

    @pl.when(i % tiles_per_seq == 0)
    def _():
        u_ref[0:SUBLANES, :] = jnp.zeros((SUBLANES, CONV_DIM), F32)

    @pl.when(i % tiles_per_seq != 0)
    def _():
        u_ref[0:SUBLANES, :] = u_ref[tm:tm + SUBLANES, :]

    u_ref[SUBLANES:SUBLANES + tm, :] = proj(_OFF_C, CONV_DIM) * proj(_OFF_H, CONV_DIM)
    y = (cw_ref[2:3, :] * u_ref[SUBLANES:SUBLANES + tm, :]
         + cw_ref[1:2, :] * u_ref[SUBLANES - 1:SUBLANES - 1 + tm, :]
         + cw_ref[0:1, :] * u_ref[SUBLANES - 2:SUBLANES - 2 + tm, :])
    c = proj(_OFF_B, CONV_DIM) * y
    cn_ref[...] = _rms(c, gc_ref[...]).astype(BF16)


def _in_proj(x, gpre, w_in, conv_w, gconv, layer, seq, *, tm=512):
    n, d = x.shape
    assert n % tm == 0 and seq % tm == 0
    vmem = (2 * tm * d * 4 + 2 * d * IN_PROJ_DIM * 2
            + 2 * tm * (Q_DIM + 2 * KV_DIM) * 4 + 2 * tm * CONV_DIM * 2
            + (tm + SUBLANES) * CONV_DIM * 4 + 4 * tm * CONV_DIM * 4)
    return pl.pallas_call(
        functools.partial(_in_proj_kernel, tiles_per_seq=seq // tm),
        grid=(n // tm,),
        in_specs=[
            pl.BlockSpec((tm, d), lambda i: (i, 0)),
            pl.BlockSpec((None, 1, d), lambda i: (layer, 0, 0)),
            pl.BlockSpec((None, d, IN_PROJ_DIM), lambda i: (layer, 0, 0)),
            pl.BlockSpec((None, CONV_WIDTH, CONV_DIM), lambda i: (layer, 0, 0)),
            pl.BlockSpec((None, 1, CONV_DIM), lambda i: (layer, 0, 0)),
        ],
        out_specs=[
            pl.BlockSpec((N_Q_HEADS, tm, HEAD_DIM), lambda i: (0, i, 0)),
            pl.BlockSpec((N_KV_HEADS, tm, HEAD_DIM), lambda i: (0, i, 0)),
            pl.BlockSpec((N_KV_HEADS, tm, HEAD_DIM), lambda i: (0, i, 0)),
            pl.BlockSpec((tm, CONV_DIM), lambda i: (i, 0)),
        ],
        out_shape=[
            jax.ShapeDtypeStruct((N_Q_HEADS, n, HEAD_DIM), F32),
            jax.ShapeDtypeStruct((N_KV_HEADS, n, HEAD_DIM), F32),
            jax.ShapeDtypeStruct((N_KV_HEADS, n, HEAD_DIM), F32),
            jax.ShapeDtypeStruct((n, CONV_DIM), BF16),
        ],
        scratch_shapes=[pltpu.VMEM((tm + SUBLANES, CONV_DIM), F32)],
        compiler_params=pltpu.CompilerParams(
            dimension_semantics=("arbitrary",),
            vmem_limit_bytes=min(vmem + (4 << 20), V7X_VMEM_BYTES - (4 << 20))),
        name="in_proj",
    )(x, gpre, w_in, conv_w, gconv)


def _attn_kernel(q_ref, k_ref, v_ref, o_ref, acc_ref, m_ref, l_ref):
    seq = k_ref.shape[0]
    scale = HEAD_DIM ** -0.5
    rows = Q_PER_KV * SPAN
    qi = lax.broadcasted_iota(jnp.int32, (rows, 2 * SPAN), 0) % SPAN
    kj = lax.broadcasted_iota(jnp.int32, (rows, 2 * SPAN), 1)
    mask_two = (kj >= qi) & (kj <= qi + SPAN)
    qi1 = lax.broadcasted_iota(jnp.int32, (rows, SPAN), 0) % SPAN
    kj1 = lax.broadcasted_iota(jnp.int32, (rows, SPAN), 1)
    mask_one = kj1 <= qi1

    def block(q_idx, k_idx, mask):
        q = jnp.concatenate([q_ref[r, q_idx, :] for r in range(Q_PER_KV)], axis=0).astype(BF16)
        k = k_ref[k_idx, :].astype(BF16)
        v = v_ref[k_idx, :].astype(BF16)
        s = lax.dot_general(q, k, (((1,), (1,)), ((), ())), preferred_element_type=F32) * scale
        s = jnp.where(mask, s, -jnp.inf)
        ms, ps, ls = [], [], []
        for r in range(Q_PER_KV):
            s_r = s[r * SPAN:(r + 1) * SPAN]
            m_r = jnp.max(s_r, axis=-1, keepdims=True)
            p_r = jnp.exp(s_r - m_r)
            ms.append(m_r)
            ls.append(jnp.sum(p_r, axis=-1, keepdims=True))
            ps.append(p_r.astype(BF16))
        o = jnp.dot(jnp.concatenate(ps, axis=0), v, preferred_element_type=F32)
        return o, ms, ls

    def init_state(q_idx, o, m, l):
        for r in range(Q_PER_KV):
            acc_ref[r, q_idx, :] = o[r * SPAN:(r + 1) * SPAN]
            m_ref[r, q_idx, :] = jnp.broadcast_to(m[r], (SPAN, HEAD_DIM))
            l_ref[r, q_idx, :] = jnp.broadcast_to(l[r], (SPAN, HEAD_DIM))

    def merged(r, q_idx, o, m, l):
        m_old = m_ref[r, q_idx, :]
        m_new = jnp.maximum(m_old, m[r])
        a_old = jnp.exp(m_old - m_new)
        a_new = jnp.exp(m[r] - m_new)
        acc = acc_ref[r, q_idx, :] * a_old + o[r * SPAN:(r + 1) * SPAN] * a_new
        den = l_ref[r, q_idx, :] * a_old + l[r] * a_new
        return acc, m_new, den

    def merge_state(q_idx, o, m, l):
        for r in range(Q_PER_KV):
            acc, m_new, den = merged(r, q_idx, o, m, l)
            acc_ref[r, q_idx, :] = acc
            m_ref[r, q_idx, :] = m_new
            l_ref[r, q_idx, :] = den

    def finish(q_idx, o, m, l):
        for r in range(Q_PER_KV):
            acc, _, den = merged(r, q_idx, o, m, l)
            o_ref[q_idx, r * HEAD_DIM:(r + 1) * HEAD_DIM] = acc / den

    def run_branch(dil, update):
        nb = seq // (dil * SPAN)

        def first_block(e, carry):
            idx = pl.ds(e, SPAN, stride=dil) if dil > 1 else pl.ds(e, SPAN)
            update(idx, *block(idx, idx, mask_one))
            return carry

        lax.fori_loop(0, dil, first_block, 0)
        if nb > 1:
            def later_block(t, carry):
                e = t // (nb - 1)
                nblk = t % (nb - 1) + 1
                start = e + nblk * (dil * SPAN)
                prev = start - dil * SPAN
                if dil > 1:
                    q_idx = pl.ds(start, SPAN, stride=dil)
                    k_idx = pl.ds(prev, 2 * SPAN, stride=dil)
                else:
                    q_idx = pl.ds(pl.multiple_of(start, SPAN), SPAN)
                    k_idx = pl.ds(pl.multiple_of(prev, SPAN), 2 * SPAN)
                update(q_idx, *block(q_idx, k_idx, mask_two))
                return carry

            lax.fori_loop(0, dil * (nb - 1), later_block, 0)

    run_branch(DILATIONS[2], init_state)
    run_branch(DILATIONS[1], merge_state)
    run_branch(DILATIONS[0], finish)


def _attention(q, k, v, batch, seq):
    n = q.shape[1]
    assert seq % (DILATIONS[2] * SPAN) == 0
    slab = Q_PER_KV * seq * HEAD_DIM * 4
    vmem = 2 * slab + 2 * 2 * seq * HEAD_DIM * 4 + 2 * slab + 3 * slab
    return pl.pallas_call(
        _attn_kernel,
        grid=(batch, N_KV_HEADS),
        in_specs=[
            pl.BlockSpec((Q_PER_KV, seq, HEAD_DIM), lambda b, g: (g, b, 0)),
            pl.BlockSpec((None, seq, HEAD_DIM), lambda b, g: (g, b, 0)),
            pl.BlockSpec((None, seq, HEAD_DIM), lambda b, g: (g, b, 0)),
        ],
        out_specs=pl.BlockSpec((seq, Q_PER_KV * HEAD_DIM), lambda b, g: (b, g)),
        out_shape=jax.ShapeDtypeStruct((n, Q_DIM), F32),
        scratch_shapes=[pltpu.VMEM((Q_PER_KV, seq, HEAD_DIM), F32)] * 3,
        compiler_params=pltpu.CompilerParams(
            dimension_semantics=("parallel", "parallel"),
            vmem_limit_bytes=min(vmem + (8 << 20), V7X_VMEM_BYTES - (4 << 20))),
        name="attn",
    )(q, k, v)


def _out_proj_kernel(a_ref, cn_ref, x_ref, ga_ref, w_ref, gpost_ref, o_ref):
    an = _rms(a_ref[...], ga_ref[...]).astype(BF16)
    mixed = jnp.dot(an, w_ref[0:Q_DIM, :], preferred_element_type=F32)
    mixed += jnp.dot(cn_ref[...], w_ref[Q_DIM:, :], preferred_element_type=F32)
    o_ref[...] = x_ref[...] + _rms(mixed, gpost_ref[...])


def _out_proj(a, cn, x, ga, w_out, gpost, layer, *, tm=512):
    n, d = x.shape
    assert n % tm == 0
    vmem = (2 * tm * Q_DIM * 4 + 2 * tm * CONV_DIM * 2 + 4 * tm * d * 4
            + 2 * (Q_DIM + CONV_DIM) * d * 2 + 2 * tm * d * 4)
    return pl.pallas_call(
        _out_proj_kernel,
        grid=(n // tm,),
        in_specs=[
            pl.BlockSpec((tm, Q_DIM), lambda i: (i, 0)),
            pl.BlockSpec((tm, CONV_DIM), lambda i: (i, 0)),
            pl.BlockSpec((tm, d), lambda i: (i, 0)),
            pl.BlockSpec((None, 1, Q_DIM), lambda i: (layer, 0, 0)),
            pl.BlockSpec((None, Q_DIM + CONV_DIM, d), lambda i: (layer, 0, 0)),
            pl.BlockSpec((None, 1, d), lambda i: (layer, 0, 0)),
        ],
        out_specs=pl.BlockSpec((tm, d), lambda i: (i, 0)),
        out_shape=jax.ShapeDtypeStruct((n, d), F32),
        compiler_params=pltpu.CompilerParams(
            dimension_semantics=("parallel",),
            vmem_limit_bytes=min(vmem + (4 << 20), V7X_VMEM_BYTES - (4 << 20))),
        name="out_proj",
    )(a, cn, x, ga, w_out, gpost)


def kernel(x, ffn1_norm_pre, ffn1_w_gate_up, ffn1_w_down, ffn1_norm_post, mix_norm_pre, w_in, conv_w, attn_out_norm, conv_out_norm, w_out, mix_norm_post, ffn2_norm_pre, ffn2_w_gate_up, ffn2_w_down, ffn2_norm_post):
    batch, seq, d = x.shape
    depth = w_in.shape[0]
    xf = x.reshape(batch * seq, d)

    def row(gain):
        return gain.reshape(gain.shape[0], 1, gain.shape[1])

    w1_gu, w1_d = ffn1_w_gate_up.astype(BF16), ffn1_w_down.astype(BF16)
    w2_gu, w2_d = ffn2_w_gate_up.astype(BF16), ffn2_w_down.astype(BF16)
    w_in_b, w_out_b = w_in.astype(BF16), w_out.astype(BF16)

    for l in range(depth):
        xf = _ffn(xf, row(ffn1_norm_pre), w1_gu, w1_d, row(ffn1_norm_post), l)
        q, k, v, cn = _in_proj(xf, row(mix_norm_pre), w_in_b, conv_w, row(conv_out_norm), l, seq)
        a = _attention(q, k, v, batch, seq)
        xf = _out_proj(a, cn, xf, row(attn_out_norm), w_out_b, row(mix_norm_post), l)
        xf = _ffn(xf, row(ffn2_norm_pre), w2_gu, w2_d, row(ffn2_norm_post), l)
    return xf.reshape(batch, seq, d)
```

```python
import functools

import jax
import jax.numpy as jnp
from jax import lax
from jax.experimental import pallas as pl
from jax.experimental.pallas import tpu as pltpu

D_MODEL = 2048
HEAD_DIM = 128
ATTN_WIDTH = D_MODEL // 2
CONV_DIM = D_MODEL - ATTN_WIDTH
N_Q_HEADS = ATTN_WIDTH // HEAD_DIM
N_KV_HEADS = max(1, N_Q_HEADS // 4)
Q_PER_KV = N_Q_HEADS // N_KV_HEADS
Q_DIM = N_Q_HEADS * HEAD_DIM
KV_DIM = N_KV_HEADS * HEAD_DIM
CONV_WIDTH = 3
IN_PROJ_DIM = Q_DIM + 2 * KV_DIM + 3 * CONV_DIM
SPAN = 128
DILATIONS = (1, 4, 16)
FFN_RESIDUAL_WEIGHT = 0.5
NORM_EPS = 1e-6

V7X_VMEM_BYTES = 64 * 1024 * 1024
SUBLANES = 8

BF16 = jnp.bfloat16
F32 = jnp.float32

_OFF_K = Q_DIM
_OFF_V = Q_DIM + KV_DIM
_OFF_H = Q_DIM + 2 * KV_DIM
_OFF_B = _OFF_H + CONV_DIM
_OFF_C = _OFF_B + CONV_DIM


def _rms(x, gain):
    y = x * lax.rsqrt(jnp.mean(x * x, axis=-1, keepdims=True) + NORM_EPS)
    return y * gain


def _ffn_kernel(x_ref, gpre_ref, wg_ref, wu_ref, wd_ref, gpost_ref, o_ref, h_ref, acc_ref):
    j = pl.program_id(1)

    @pl.when(j == 0)
    def _():
        h_ref[...] = _rms(x_ref[...], gpre_ref[...]).astype(BF16)
        acc_ref[...] = jnp.zeros_like(acc_ref)

    h = h_ref[...]
    g = jnp.dot(h, wg_ref[...], preferred_element_type=F32)
    u = jnp.dot(h, wu_ref[...], preferred_element_type=F32)
    a = (g * jax.nn.sigmoid(g) * u).astype(BF16)
    acc_ref[...] += jnp.dot(a, wd_ref[...], preferred_element_type=F32)

    @pl.when(j == pl.num_programs(1) - 1)
    def _():
        o_ref[...] = x_ref[...] + FFN_RESIDUAL_WEIGHT * _rms(acc_ref[...], gpost_ref[...])


def _ffn(x, gpre, w_gu, w_down, gpost, layer, *, tm=512, tf=512):
    n, d = x.shape
    d_ff = w_down.shape[1]
    nj = d_ff // tf
    assert n % tm == 0 and d_ff % tf == 0
    vmem = (2 * 2 * tm * d * 4
            + tm * d * (2 + 4)
            + 2 * 3 * d * tf * 2
            + 4 * tm * tf * 4
            + tm * d * 4)
    return pl.pallas_call(
        _ffn_kernel,
        grid=(n // tm, nj),
        in_specs=[
            pl.BlockSpec((tm, d), lambda i, j: (i, 0)),
            pl.BlockSpec((None, 1, d), lambda i, j: (layer, 0, 0)),
            pl.BlockSpec((None, d, tf), lambda i, j: (layer, 0, j)),
            pl.BlockSpec((None, d, tf), lambda i, j: (layer, 0, j + nj)),
            pl.BlockSpec((None, tf, d), lambda i, j: (layer, j, 0)),
            pl.BlockSpec((None, 1, d), lambda i, j: (layer, 0, 0)),
        ],
        out_specs=pl.BlockSpec((tm, d), lambda i, j: (i, 0)),
        out_shape=jax.ShapeDtypeStruct((n, d), F32),
        scratch_shapes=[pltpu.VMEM((tm, d), BF16), pltpu.VMEM((tm, d), F32)],
        compiler_params=pltpu.CompilerParams(
            dimension_semantics=("parallel", "arbitrary"),
            vmem_limit_bytes=min(vmem + (4 << 20), V7X_VMEM_BYTES - (4 << 20))),
        name="ffn",
    )(x, gpre, w_gu, w_gu, w_down, gpost)


def _in_proj_kernel(x_ref, gpre_ref, w_ref, cw_ref, gc_ref, q_ref, k_ref, v_ref, cn_ref,
                    u_ref, *, tiles_per_seq):
    i = pl.program_id(0)
    tm = x_ref.shape[0]
    h = _rms(x_ref[...], gpre_ref[...]).astype(BF16)

    def proj(off, width):
        return jnp.dot(h, w_ref[:, off:off + width], preferred_element_type=F32)

    zq = proj(0, Q_DIM)
    for hd in range(N_Q_HEADS):
        q_ref[hd] = zq[:, hd * HEAD_DIM:(hd + 1) * HEAD_DIM]
    zk = proj(_OFF_K, KV_DIM)
    zv = proj(_OFF_V, KV_DIM)
    for hd in range(N_KV_HEADS):
        k_ref[hd] = zk[:, hd * HEAD_DIM:(hd + 1) * HEAD_DIM]
        v_ref[hd] = zv[:, hd * HEAD_DIM:(hd + 1) * HEAD_DIM]

    @pl.when(i % tiles_per_seq == 0)
    def _():
        u_ref[0:SUBLANES, :] = jnp.zeros((SUBLANES, CONV_DIM), F32)

    @pl.when(i % tiles_per_seq != 0)
    def _():
        u_ref[0:SUBLANES, :] = u_ref[tm:tm + SUBLANES, :]

    u_ref[SUBLANES:SUBLANES + tm, :] = proj(_OFF_C, CONV_DIM) * proj(_OFF_H, CONV_DIM)
    y = (cw_ref[2:3, :] * u_ref[SUBLANES:SUBLANES + tm, :]
         + cw_ref[1:2, :] * u_ref[SUBLANES - 1:SUBLANES - 1 + tm, :]
         + cw_ref[0:1, :] * u_ref[SUBLANES - 2:SUBLANES - 2 + tm, :])
    c = proj(_OFF_B, CONV_DIM) * y
    cn_ref[...] = _rms(c, gc_ref[...]).astype(BF16)


def _in_proj(x, gpre, w_in, conv_w, gconv, layer, seq, *, tm=512):
    n, d = x.shape
    assert n % tm == 0 and seq % tm == 0
    vmem = (2 * tm * d * 4 + 2 * d * IN_PROJ_DIM * 2
            + 2 * tm * (Q_DIM + 2 * KV_DIM) * 4 + 2 * tm * CONV_DIM * 2
            + (tm + SUBLANES) * CONV_DIM * 4 + 4 * tm * CONV_DIM * 4)
    return pl.pallas_call(
        functools.partial(_in_proj_kernel, tiles_per_seq=seq // tm),
        grid=(n // tm,),
        in_specs=[
            pl.BlockSpec((tm, d), lambda i: (i, 0)),
            pl.BlockSpec((None, 1, d), lambda i: (layer, 0, 0)),
            pl.BlockSpec((None, d, IN_PROJ_DIM), lambda i: (layer, 0, 0)),
            pl.BlockSpec((None, CONV_WIDTH, CONV_DIM), lambda i: (layer, 0, 0)),
            pl.BlockSpec((None, 1, CONV_DIM), lambda i: (layer, 0, 0)),
        ],
        out_specs=[
            pl.BlockSpec((N_Q_HEADS, tm, HEAD_DIM), lambda i: (0, i, 0)),
            pl.BlockSpec((N_KV_HEADS, tm, HEAD_DIM), lambda i: (0, i, 0)),
            pl.BlockSpec((N_KV_HEADS, tm, HEAD_DIM), lambda i: (0, i, 0)),
            pl.BlockSpec((tm, CONV_DIM), lambda i: (i, 0)),
        ],
        out_shape=[
            jax.ShapeDtypeStruct((N_Q_HEADS, n, HEAD_DIM), F32),
            jax.ShapeDtypeStruct((N_KV_HEADS, n, HEAD_DIM), F32),
            jax.ShapeDtypeStruct((N_KV_HEADS, n, HEAD_DIM), F32),
            jax.ShapeDtypeStruct((n, CONV_DIM), BF16),
        ],
        scratch_shapes=[pltpu.VMEM((tm + SUBLANES, CONV_DIM), F32)],
        compiler_params=pltpu.CompilerParams(
            dimension_semantics=("arbitrary",),
            vmem_limit_bytes=min(vmem + (4 << 20), V7X_VMEM_BYTES - (4 << 20))),
        name="in_proj",
    )(x, gpre, w_in, conv_w, gconv)


def _attn_kernel(q_ref, k_ref, v_ref, o_ref, acc_ref, m_ref, l_ref):
    seq = k_ref.shape[0]
    scale = HEAD_DIM ** -0.5
    rows = Q_PER_KV * SPAN
    qi = lax.broadcasted_iota(jnp.int32, (rows, 2 * SPAN), 0) % SPAN
    kj = lax.broadcasted_iota(jnp.int32, (rows, 2 * SPAN), 1)
    mask_two = (kj >= qi) & (kj <= qi + SPAN)
    qi1 = lax.broadcasted_iota(jnp.int32, (rows, SPAN), 0) % SPAN
    kj1 = lax.broadcasted_iota(jnp.int32, (rows, SPAN), 1)
    mask_one = kj1 <= qi1

    def block(q_idx, k_idx, mask):
        q = jnp.concatenate([q_ref[r, q_idx, :] for r in range(Q_PER_KV)], axis=0).astype(BF16)
        k = k_ref[k_idx, :].astype(BF16)
        v = v_ref[k_idx, :].astype(BF16)
        s = lax.dot_general(q, k, (((1,), (1,)), ((), ())), preferred_element_type=F32) * scale
        s = jnp.where(mask, s, -jnp.inf)
        ms, ps, ls = [], [], []
        for r in range(Q_PER_KV):
            s_r = s[r * SPAN:(r + 1) * SPAN]
            m_r = jnp.max(s_r, axis=-1, keepdims=True)
            p_r = jnp.exp(s_r - m_r)
            ms.append(m_r)
            ls.append(jnp.sum(p_r, axis=-1, keepdims=True))
            ps.append(p_r.astype(BF16))
        o = jnp.dot(jnp.concatenate(ps, axis=0), v, preferred_element_type=F32)
        return o, ms, ls

    def init_state(q_idx, o, m, l):
        for r in range(Q_PER_KV):
            acc_ref[r, q_idx, :] = o[r * SPAN:(r + 1) * SPAN]
            m_ref[r, q_idx, :] = jnp.broadcast_to(m[r], (SPAN, HEAD_DIM))
            l_ref[r, q_idx, :] = jnp.broadcast_to(l[r], (SPAN, HEAD_DIM))

    def merged(r, q_idx, o, m, l):
        m_old = m_ref[r, q_idx, :]
        m_new = jnp.maximum(m_old, m[r])
        a_old = jnp.exp(m_old - m_new)
        a_new = jnp.exp(m[r] - m_new)
        acc = acc_ref[r, q_idx, :] * a_old + o[r * SPAN:(r + 1) * SPAN] * a_new
        den = l_ref[r, q_idx, :] * a_old + l[r] * a_new
        return acc, m_new, den

    def merge_state(q_idx, o, m, l):
        for r in range(Q_PER_KV):
            acc, m_new, den = merged(r, q_idx, o, m, l)
            acc_ref[r, q_idx, :] = acc
            m_ref[r, q_idx, :] = m_new
            l_ref[r, q_idx, :] = den

    def finish(q_idx, o, m, l):
        for r in range(Q_PER_KV):
            acc, _, den = merged(r, q_idx, o, m, l)
            o_ref[q_idx, r * HEAD_DIM:(r + 1) * HEAD_DIM] = acc / den

    def run_branch(dil, update):
        nb = seq // (dil * SPAN)

        def first_block(e, carry):
            idx = pl.ds(e, SPAN, stride=dil) if dil > 1 else pl.ds(e, SPAN)
            update(idx, *block(idx, idx, mask_one))
            return carry

        if dil > 1:
            lax.fori_loop(0, dil, first_block, 0, unroll=2)
        else:
            first_block(0, 0)
        if nb > 1:
            def later_block(t, carry):
                e = t // (nb - 1)
                nblk = t % (nb - 1) + 1
                start = e + nblk * (dil * SPAN)
                prev = start - dil * SPAN
                if dil > 1:
                    q_idx = pl.ds(start, SPAN, stride=dil)
                    k_idx = pl.ds(prev, 2 * SPAN, stride=dil)
                else:
                    q_idx = pl.ds(pl.multiple_of(start, SPAN), SPAN)
                    k_idx = pl.ds(pl.multiple_of(prev, SPAN), 2 * SPAN)
                update(q_idx, *block(q_idx, k_idx, mask_two))
                return carry

            lax.fori_loop(0, dil * (nb - 1), later_block, 0, unroll=2)

    run_branch(DILATIONS[2], init_state)
    run_branch(DILATIONS[1], merge_state)
    run_branch(DILATIONS[0], finish)


def _attention(q, k, v, batch, seq):
    n = q.shape[1]
    assert seq % (DILATIONS[2] * SPAN) == 0
    slab = Q_PER_KV * seq * HEAD_DIM * 4
    vmem = 2 * slab + 2 * 2 * seq * HEAD_DIM * 4 + 2 * slab + 3 * slab
    return pl.pallas_call(
        _attn_kernel,
        grid=(batch, N_KV_HEADS),
        in_specs=[
            pl.BlockSpec((Q_PER_KV, seq, HEAD_DIM), lambda b, g: (g, b, 0)),
            pl.BlockSpec((None, seq, HEAD_DIM), lambda b, g: (g, b, 0)),
            pl.BlockSpec((None, seq, HEAD_DIM), lambda b, g: (g, b, 0)),
        ],
        out_specs=pl.BlockSpec((seq, Q_PER_KV * HEAD_DIM), lambda b, g: (b, g)),
        out_shape=jax.ShapeDtypeStruct((n, Q_DIM), F32),
        scratch_shapes=[pltpu.VMEM((Q_PER_KV, seq, HEAD_DIM), F32)] * 3,
        compiler_params=pltpu.CompilerParams(
            dimension_semantics=("parallel", "parallel"),
            vmem_limit_bytes=min(vmem + (8 << 20), V7X_VMEM_BYTES - (4 << 20))),
        name="attn",
    )(q, k, v)


def _out_proj_kernel(a_ref, cn_ref, x_ref, ga_ref, w_ref, gpost_ref, o_ref):
    an = _rms(a_ref[...], ga_ref[...]).astype(BF16)
    mixed = jnp.dot(an, w_ref[0:Q_DIM, :], preferred_element_type=F32)
    mixed += jnp.dot(cn_ref[...], w_ref[Q_DIM:, :], preferred_element_type=F32)
    o_ref[...] = x_ref[...] + _rms(mixed, gpost_ref[...])


def _out_proj(a, cn, x, ga, w_out, gpost, layer, *, tm=512):
    n, d = x.shape
    assert n % tm == 0
    vmem = (2 * tm * Q_DIM * 4 + 2 * tm * CONV_DIM * 2 + 4 * tm * d * 4
            + 2 * (Q_DIM + CONV_DIM) * d * 2 + 2 * tm * d * 4)
    return pl.pallas_call(
        _out_proj_kernel,
        grid=(n // tm,),
        in_specs=[
            pl.BlockSpec((tm, Q_DIM), lambda i: (i, 0)),
            pl.BlockSpec((tm, CONV_DIM), lambda i: (i, 0)),
            pl.BlockSpec((tm, d), lambda i: (i, 0)),
            pl.BlockSpec((None, 1, Q_DIM), lambda i: (layer, 0, 0)),
            pl.BlockSpec((None, Q_DIM + CONV_DIM, d), lambda i: (layer, 0, 0)),
            pl.BlockSpec((None, 1, d), lambda i: (layer, 0, 0)),
        ],
        out_specs=pl.BlockSpec((tm, d), lambda i: (i, 0)),
        out_shape=jax.ShapeDtypeStruct((n, d), F32),
        compiler_params=pltpu.CompilerParams(
            dimension_semantics=("parallel",),
            vmem_limit_bytes=min(vmem + (4 << 20), V7X_VMEM_BYTES - (4 << 20))),
        name="out_proj",
    )(a, cn, x, ga, w_out, gpost)


def kernel(x, ffn1_norm_pre, ffn1_w_gate_up, ffn1_w_down, ffn1_norm_post, mix_norm_pre, w_in, conv_w, attn_out_norm, conv_out_norm, w_out, mix_norm_post, ffn2_norm_pre, ffn2_w_gate_up, ffn2_w_down, ffn2_norm_post):
    batch, seq, d = x.shape
    depth = w_in.shape[0]
    xf = x.reshape(batch * seq, d)

    def row(gain):
        return gain.reshape(gain.shape[0], 1, gain.shape[1])

    w1_gu, w1_d = ffn1_w_gate_up.astype(BF16), ffn1_w_down.astype(BF16)
    w2_gu, w2_d = ffn2_w_gate_up.astype(BF16), ffn2_w_down.astype(BF16)
    w_in_b, w_out_b = w_in.astype(BF16), w_out.astype(BF16)

    for l in range(depth):
        xf = _ffn(xf, row(ffn1_norm_pre), w1_gu, w1_d, row(ffn1_norm_post), l)
        q, k, v, cn = _in_proj(xf, row(mix_norm_pre), w_in_b, conv_w, row(conv_out_norm), l, seq)
        a = _attention(q, k, v, batch, seq)
        xf = _out_proj(a, cn, xf, row(attn_out_norm), w_out_b, row(mix_norm_post), l)
        xf = _ffn(xf, row(ffn2_norm_pre), w2_gu, w2_d, row(ffn2_norm_post), l)
    return xf.reshape(batch, seq, d)
```

```python
import functools

import jax
import jax.numpy as jnp
from jax import lax
from jax.experimental import pallas as pl
from jax.experimental.pallas import tpu as pltpu

D_MODEL = 2048
HEAD_DIM = 128
ATTN_WIDTH = D_MODEL // 2
CONV_DIM = D_MODEL - ATTN_WIDTH
N_Q_HEADS = ATTN_WIDTH // HEAD_DIM
N_KV_HEADS = max(1, N_Q_HEADS // 4)
Q_PER_KV = N_Q_HEADS // N_KV_HEADS
Q_DIM = N_Q_HEADS * HEAD_DIM
KV_DIM = N_KV_HEADS * HEAD_DIM
CONV_WIDTH = 3
IN_PROJ_DIM = Q_DIM + 2 * KV_DIM + 3 * CONV_DIM
SPAN = 128
DILATIONS = (1, 4, 16)
FFN_RESIDUAL_WEIGHT = 0.5
NORM_EPS = 1e-6
LOG2_E = 1.4426950408889634

V7X_VMEM_BYTES = 64 * 1024 * 1024
SUBLANES = 8
FFN_ROW_SUBTILE = 512
NORM_ROW_CHUNK = 128

BF16 = jnp.bfloat16
F32 = jnp.float32

_OFF_K = Q_DIM
_OFF_V = Q_DIM + KV_DIM
_OFF_H = Q_DIM + 2 * KV_DIM
_OFF_B = _OFF_H + CONV_DIM
_OFF_C = _OFF_B + CONV_DIM


def _rms(x, gain):
    y = x * lax.rsqrt(jnp.mean(x * x, axis=-1, keepdims=True) + NORM_EPS)
    return y * gain


def _for_row_chunks(n_rows, body):
    for r0 in range(0, n_rows, NORM_ROW_CHUNK):
        body(slice(r0, r0 + NORM_ROW_CHUNK))


def _ffn_kernel(x_ref, gpre_ref, wg_ref, wu_ref, wd_ref, gpost_ref, o_ref, h_ref):
    j = pl.program_id(1)

    @pl.when(j == 0)
    def _():
        def pre(rows):
            h_ref[rows, :] = _rms(x_ref[rows, :], gpre_ref[...]).astype(BF16)
            o_ref[rows, :] = jnp.zeros((NORM_ROW_CHUNK, o_ref.shape[1]), F32)

        _for_row_chunks(h_ref.shape[0], pre)

    for r0 in range(0, h_ref.shape[0], FFN_ROW_SUBTILE):
        rows = slice(r0, r0 + FFN_ROW_SUBTILE)
        h = h_ref[rows, :]
        g = jnp.dot(h, wg_ref[...], preferred_element_type=F32)
        u = jnp.dot(h, wu_ref[...], preferred_element_type=F32)
        a = (g * jax.nn.sigmoid(g) * u).astype(BF16)
        o_ref[rows, :] += jnp.dot(a, wd_ref[...], preferred_element_type=F32)

    @pl.when(j == pl.num_programs(1) - 1)
    def _():
        half_gain = FFN_RESIDUAL_WEIGHT * gpost_ref[...]

        def post(rows):
            o_ref[rows, :] = x_ref[rows, :] + _rms(o_ref[rows, :], half_gain)

        _for_row_chunks(h_ref.shape[0], post)


def _ffn(x, gpre, w_gu, w_down, gpost, layer, *, tm=1024, tf=512):
    n, d = x.shape
    d_ff = w_down.shape[1]
    nj = d_ff // tf
    assert n % tm == 0 and d_ff % tf == 0
    vmem = (2 * 2 * tm * d * 4
            + tm * d * 2
            + 2 * 3 * d * tf * 2
            + 6 * FFN_ROW_SUBTILE * tf * 4)
    return pl.pallas_call(
        _ffn_kernel,
        grid=(n // tm, nj),
        in_specs=[
            pl.BlockSpec((tm, d), lambda i, j: (i, 0)),
            pl.BlockSpec((None, 1, d), lambda i, j: (layer, 0, 0)),
            pl.BlockSpec((None, d, tf), lambda i, j: (layer, 0, j)),
            pl.BlockSpec((None, d, tf), lambda i, j: (layer, 0, j + nj)),
            pl.BlockSpec((None, tf, d), lambda i, j: (layer, j, 0)),
            pl.BlockSpec((None, 1, d), lambda i, j: (layer, 0, 0)),
        ],
        out_specs=pl.BlockSpec((tm, d), lambda i, j: (i, 0)),
        out_shape=jax.ShapeDtypeStruct((n, d), F32),
        scratch_shapes=[pltpu.VMEM((tm, d), BF16)],
        compiler_params=pltpu.CompilerParams(
            dimension_semantics=("parallel", "arbitrary"),
            vmem_limit_bytes=min(vmem + (4 << 20), V7X_VMEM_BYTES - (4 << 20))),
        name="ffn",
    )(x, gpre, w_gu, w_gu, w_down, gpost)


def _in_proj_kernel(x_ref, gpre_ref, w_ref, cw_ref, gc_ref, q_ref, k_ref, v_ref, cn_ref,
                    u_ref, *, tiles_per_seq):
    i = pl.program_id(0)
    tm = x_ref.shape[0]
    h = _rms(x_ref[...], gpre_ref[...]).astype(BF16)

    def proj(off, width):
        return jnp.dot(h, w_ref[:, off:off + width], preferred_element_type=F32)

    zq = proj(0, Q_DIM)
    for hd in range(N_Q_HEADS):
        q_ref[hd] = zq[:, hd * HEAD_DIM:(hd + 1) * HEAD_DIM]
    zk = proj(_OFF_K, KV_DIM)
    zv = proj(_OFF_V, KV_DIM)
    for hd in range(N_KV_HEADS):
        k_ref[hd] = zk[:, hd * HEAD_DIM:(hd + 1) * HEAD_DIM]
        v_ref[hd] = zv[:, hd * HEAD_DIM:(hd + 1) * HEAD_DIM]

    @pl.when(i % tiles_per_seq == 0)
    def _():
        u_ref[0:SUBLANES, :] = jnp.zeros((SUBLANES, CONV_DIM), F32)

    @pl.when(i % tiles_per_seq != 0)
    def _():
        u_ref[0:SUBLANES, :] = u_ref[tm:tm + SUBLANES, :]

    u_ref[SUBLANES:SUBLANES + tm, :] = proj(_OFF_C, CONV_DIM) * proj(_OFF_H, CONV_DIM)
    y = (cw_ref[2:3, :] * u_ref[SUBLANES:SUBLANES + tm, :]
         + cw_ref[1:2, :] * u_ref[SUBLANES - 1:SUBLANES - 1 + tm, :]
         + cw_ref[0:1, :] * u_ref[SUBLANES - 2:SUBLANES - 2 + tm, :])
    c = proj(_OFF_B, CONV_DIM) * y
    cn_ref[...] = _rms(c, gc_ref[...]).astype(BF16)


def _in_proj(x, gpre, w_in, conv_w, gconv, layer, seq, *, tm=512):
    n, d = x.shape
    assert n % tm == 0 and seq % tm == 0
    vmem = (2 * tm * d * 4 + 2 * d * IN_PROJ_DIM * 2
            + 2 * tm * (Q_DIM + 2 * KV_DIM) * 4 + 2 * tm * CONV_DIM * 2
            + (tm + SUBLANES) * CONV_DIM * 4 + 4 * tm * CONV_DIM * 4)
    return pl.pallas_call(
        functools.partial(_in_proj_kernel, tiles_per_seq=seq // tm),
        grid=(n // tm,),
        in_specs=[
            pl.BlockSpec((tm, d), lambda i: (i, 0)),
            pl.BlockSpec((None, 1, d), lambda i: (layer, 0, 0)),
            pl.BlockSpec((None, d, IN_PROJ_DIM), lambda i: (layer, 0, 0)),
            pl.BlockSpec((None, CONV_WIDTH, CONV_DIM), lambda i: (layer, 0, 0)),
            pl.BlockSpec((None, 1, CONV_DIM), lambda i: (layer, 0, 0)),
        ],
        out_specs=[
            pl.BlockSpec((N_Q_HEADS, tm, HEAD_DIM), lambda i: (0, i, 0)),
            pl.BlockSpec((N_KV_HEADS, tm, HEAD_DIM), lambda i: (0, i, 0)),
            pl.BlockSpec((N_KV_HEADS, tm, HEAD_DIM), lambda i: (0, i, 0)),
            pl.BlockSpec((tm, CONV_DIM), lambda i: (i, 0)),
        ],
        out_shape=[
            jax.ShapeDtypeStruct((N_Q_HEADS, n, HEAD_DIM), F32),
            jax.ShapeDtypeStruct((N_KV_HEADS, n, HEAD_DIM), F32),
            jax.ShapeDtypeStruct((N_KV_HEADS, n, HEAD_DIM), F32),
            jax.ShapeDtypeStruct((n, CONV_DIM), BF16),
        ],
        scratch_shapes=[pltpu.VMEM((tm + SUBLANES, CONV_DIM), F32)],
        compiler_params=pltpu.CompilerParams(
            dimension_semantics=("arbitrary",),
            vmem_limit_bytes=min(vmem + (4 << 20), V7X_VMEM_BYTES - (4 << 20))),
        name="in_proj",
    )(x, gpre, w_in, conv_w, gconv)


def _attn_kernel(q_ref, k_ref, v_ref, o_ref, acc_ref, m_ref, l_ref):
    seq = k_ref.shape[0]
    logit_scale = (HEAD_DIM ** -0.5) * LOG2_E
    rows = Q_PER_KV * SPAN
    qi = lax.broadcasted_iota(jnp.int32, (rows, 2 * SPAN), 0) % SPAN
    kj = lax.broadcasted_iota(jnp.int32, (rows, 2 * SPAN), 1)
    mask_two = (kj >= qi) & (kj <= qi + SPAN)
    qi1 = lax.broadcasted_iota(jnp.int32, (rows, SPAN), 0) % SPAN
    kj1 = lax.broadcasted_iota(jnp.int32, (rows, SPAN), 1)
    mask_one = kj1 <= qi1

    def block(q_idx, k_idx, mask, first_branch, last_branch):
        width = mask.shape[1]
        q = jnp.concatenate([q_ref[r, q_idx, :] for r in range(Q_PER_KV)], axis=0).astype(BF16)
        k = k_ref[k_idx, :].astype(BF16)
        v = v_ref[k_idx, :].astype(BF16)
        v_ones = jnp.concatenate([v, jnp.ones((width, HEAD_DIM), BF16)], axis=1)
        t = lax.dot_general(q, k, (((1,), (1,)), ((), ())), preferred_element_type=F32)
        t = jnp.where(mask, t * logit_scale, -jnp.inf)
        ps, m_news, a_olds = [], [], []
        for r in range(Q_PER_KV):
            t_r = t[r * SPAN:(r + 1) * SPAN]
            m_blk = jnp.broadcast_to(jnp.max(t_r, axis=-1, keepdims=True), (SPAN, HEAD_DIM))
            if first_branch:
                m_new = m_blk
            else:
                m_old = m_ref[r, q_idx, :]
                m_new = jnp.maximum(m_old, m_blk)
                a_olds.append(jnp.exp2(m_old - m_new))
            m_news.append(m_new)
            p_r = [jnp.exp2(t_r[:, c * HEAD_DIM:(c + 1) * HEAD_DIM] - m_new)
                   for c in range(width // HEAD_DIM)]
            ps.append(jnp.concatenate(p_r, axis=1).astype(BF16))
        o = jnp.dot(jnp.concatenate(ps, axis=0), v_ones, preferred_element_type=F32)
        for r in range(Q_PER_KV):
            acc = o[r * SPAN:(r + 1) * SPAN, :HEAD_DIM]
            den = o[r * SPAN:(r + 1) * SPAN, HEAD_DIM:]
            if not first_branch:
                acc = acc_ref[r, q_idx, :] * a_olds[r] + acc
                den = l_ref[r, q_idx, :] * a_olds[r] + den
            if last_branch:
                o_ref[q_idx, r * HEAD_DIM:(r + 1) * HEAD_DIM] = acc / den
            else:
                acc_ref[r, q_idx, :] = acc
                m_ref[r, q_idx, :] = m_news[r]
                l_ref[r, q_idx, :] = den

    def run_branch(dil, first_branch, last_branch):
        nb = seq // (dil * SPAN)

        def first_block(e, carry):
            idx = pl.ds(e, SPAN, stride=dil) if dil > 1 else pl.ds(e, SPAN)
            block(idx, idx, mask_one, first_branch, last_branch)
            return carry

        if dil > 1:
            lax.fori_loop(0, dil, first_block, 0, unroll=2)
        else:
            first_block(0, 0)
        if nb > 1:
            def later_block(t, carry):
                e = t // (nb - 1)
                nblk = t % (nb - 1) + 1
                start = e + nblk * (dil * SPAN)
                prev = start - dil * SPAN
                if dil > 1:
                    q_idx = pl.ds(start, SPAN, stride=dil)
                    k_idx = pl.ds(prev, 2 * SPAN, stride=dil)
                else:
                    q_idx = pl.ds(pl.multiple_of(start, SPAN), SPAN)
                    k_idx = pl.ds(pl.multiple_of(prev, SPAN), 2 * SPAN)
                block(q_idx, k_idx, mask_two, first_branch, last_branch)
                return carry

            lax.fori_loop(0, dil * (nb - 1), later_block, 0, unroll=2)

    run_branch(DILATIONS[2], True, False)
    run_branch(DILATIONS[1], False, False)
    run_branch(DILATIONS[0], False, True)


def _attention(q, k, v, batch, seq):
    n = q.shape[1]
    assert seq % (DILATIONS[2] * SPAN) == 0
    slab = Q_PER_KV * seq * HEAD_DIM * 4
    vmem = 2 * slab + 2 * 2 * seq * HEAD_DIM * 4 + 2 * slab + 3 * slab
    return pl.pallas_call(
        _attn_kernel,
        grid=(batch, N_KV_HEADS),
        in_specs=[
            pl.BlockSpec((Q_PER_KV, seq, HEAD_DIM), lambda b, g: (g, b, 0)),
            pl.BlockSpec((None, seq, HEAD_DIM), lambda b, g: (g, b, 0)),
            pl.BlockSpec((None, seq, HEAD_DIM), lambda b, g: (g, b, 0)),
        ],
        out_specs=pl.BlockSpec((seq, Q_PER_KV * HEAD_DIM), lambda b, g: (b, g)),
        out_shape=jax.ShapeDtypeStruct((n, Q_DIM), F32),
        scratch_shapes=[pltpu.VMEM((Q_PER_KV, seq, HEAD_DIM), F32)] * 3,
        compiler_params=pltpu.CompilerParams(
            dimension_semantics=("parallel", "parallel"),
            vmem_limit_bytes=min(vmem + (8 << 20), V7X_VMEM_BYTES - (4 << 20))),
        name="attn",
    )(q, k, v)


def _out_proj_kernel(a_ref, cn_ref, x_ref, ga_ref, w_ref, gpost_ref, o_ref):
    an = _rms(a_ref[...], ga_ref[...]).astype(BF16)
    mixed = jnp.dot(an, w_ref[0:Q_DIM, :], preferred_element_type=F32)
    mixed += jnp.dot(cn_ref[...], w_ref[Q_DIM:, :], preferred_element_type=F32)
    o_ref[...] = x_ref[...] + _rms(mixed, gpost_ref[...])


def _out_proj(a, cn, x, ga, w_out, gpost, layer, *, tm=512):
    n, d = x.shape
    assert n % tm == 0
    vmem = (2 * tm * Q_DIM * 4 + 2 * tm * CONV_DIM * 2 + 4 * tm * d * 4
            + 2 * (Q_DIM + CONV_DIM) * d * 2 + 2 * tm * d * 4)
    return pl.pallas_call(
        _out_proj_kernel,
        grid=(n // tm,),
        in_specs=[
            pl.BlockSpec((tm, Q_DIM), lambda i: (i, 0)),
            pl.BlockSpec((tm, CONV_DIM), lambda i: (i, 0)),
            pl.BlockSpec((tm, d), lambda i: (i, 0)),
            pl.BlockSpec((None, 1, Q_DIM), lambda i: (layer, 0, 0)),
            pl.BlockSpec((None, Q_DIM + CONV_DIM, d), lambda i: (layer, 0, 0)),
            pl.BlockSpec((None, 1, d), lambda i: (layer, 0, 0)),
        ],
        out_specs=pl.BlockSpec((tm, d), lambda i: (i, 0)),
        out_shape=jax.ShapeDtypeStruct((n, d), F32),
        compiler_params=pltpu.CompilerParams(
            dimension_semantics=("parallel",),
            vmem_limit_bytes=min(vmem + (4 << 20), V7X_VMEM_BYTES - (4 << 20))),
        name="out_proj",
    )(a, cn, x, ga, w_out, gpost)


def kernel(x, ffn1_norm_pre, ffn1_w_gate_up, ffn1_w_down, ffn1_norm_post, mix_norm_pre, w_in, conv_w, attn_out_norm, conv_out_norm, w_out, mix_norm_post, ffn2_norm_pre, ffn2_w_gate_up, ffn2_w_down, ffn2_norm_post):
    batch, seq, d = x.shape
    depth = w_in.shape[0]
    xf = x.reshape(batch * seq, d)

    def row(gain):
        return gain.reshape(gain.shape[0], 1, gain.shape[1])

    w1_gu, w1_d = ffn1_w_gate_up.astype(BF16), ffn1_w_down.astype(BF16)
    w2_gu, w2_d = ffn2_w_gate_up.astype(BF16), ffn2_w_down.astype(BF16)
    w_in_b, w_out_b = w_in.astype(BF16), w_out.astype(BF16)

    for l in range(depth):
        xf = _ffn(xf, row(ffn1_norm_pre), w1_gu, w1_d, row(ffn1_norm_post), l)
        q, k, v, cn = _in_proj(xf, row(mix_norm_pre), w_in_b, conv_w, row(conv_out_norm), l, seq)
        a = _attention(q, k, v, batch, seq)
        xf = _out_proj(a, cn, xf, row(attn_out_norm), w_out_b, row(mix_norm_post), l)
        xf = _ffn(xf, row(ffn2_norm_pre), w2_gu, w2_d, row(ffn2_norm_post), l)
    return xf.reshape(batch, seq, d)
```

```python
import functools
from typing import Callable, NamedTuple

import jax
import jax.numpy as jnp
from jax import lax
from jax.experimental import pallas as pl
from jax.experimental.pallas import tpu as pltpu

D_MODEL = 2048
HEAD_DIM = 128
ATTN_WIDTH = D_MODEL // 2
CONV_DIM = D_MODEL - ATTN_WIDTH
N_Q_HEADS = ATTN_WIDTH // HEAD_DIM
N_KV_HEADS = max(1, N_Q_HEADS // 4)
Q_PER_KV = N_Q_HEADS // N_KV_HEADS
Q_DIM = N_Q_HEADS * HEAD_DIM
KV_DIM = N_KV_HEADS * HEAD_DIM
CONV_WIDTH = 3
IN_PROJ_DIM = Q_DIM + 2 * KV_DIM + 3 * CONV_DIM
SPAN = 128
DILATIONS = (1, 4, 16)
FFN_RESIDUAL_WEIGHT = 0.5
NORM_EPS = 1e-6
LOG2_E = 1.4426950408889634

V7X_VMEM_BYTES = 64 * 1024 * 1024
SUBLANES = 8
FFN_TOKEN_TILE = 1024
FFN_HIDDEN_TILE = 512
FFN_ROW_SUBTILE = 512
IN_PROJ_CAST_COLS = 512
OUT_PROJ_CAST_COLS = 256
NORM_ROW_CHUNK = 128
ATTN_UNROLL = 4

BF16 = jnp.bfloat16
F32 = jnp.float32

_OFF_K = Q_DIM
_OFF_V = Q_DIM + KV_DIM
_OFF_H = Q_DIM + 2 * KV_DIM
_OFF_B = _OFF_H + CONV_DIM
_OFF_C = _OFF_B + CONV_DIM


def _rms(x, gain):
    y = x * lax.rsqrt(jnp.mean(x * x, axis=-1, keepdims=True) + NORM_EPS)
    return y * gain


def _for_row_chunks(n_rows, body):
    for r0 in range(0, n_rows, NORM_ROW_CHUNK):
        body(slice(r0, r0 + NORM_ROW_CHUNK))


def _ffn_kernel(*refs, n_casts):
    x_ref, gpre_ref, wg_ref, wu_ref, wd_ref, gpost_ref = refs[:6]
    cast_src = refs[6:6 + n_casts]
    o_ref = refs[6 + n_casts]
    cast_dst = refs[7 + n_casts:7 + 2 * n_casts]
    h_ref = refs[7 + 2 * n_casts]
    j = pl.program_id(1)

    @pl.when(j == 0)
    def _():
        def pre(rows):
            h_ref[rows, :] = _rms(x_ref[rows, :], gpre_ref[...]).astype(BF16)
            o_ref[rows, :] = jnp.zeros((NORM_ROW_CHUNK, o_ref.shape[1]), F32)

        _for_row_chunks(h_ref.shape[0], pre)

    for r0 in range(0, h_ref.shape[0], FFN_ROW_SUBTILE):
        rows = slice(r0, r0 + FFN_ROW_SUBTILE)
        h = h_ref[rows, :]
        g = jnp.dot(h, wg_ref[...], preferred_element_type=F32)
        u = jnp.dot(h, wu_ref[...], preferred_element_type=F32)
        a = (g * jax.nn.sigmoid(g) * u).astype(BF16)
        o_ref[rows, :] += jnp.dot(a, wd_ref[...], preferred_element_type=F32)
    for src, dst in zip(cast_src, cast_dst):
        dst[...] = src[...].astype(BF16)

    @pl.when(j == pl.num_programs(1) - 1)
    def _():
        half_gain = FFN_RESIDUAL_WEIGHT * gpost_ref[...]

        def post(rows):
            o_ref[rows, :] = x_ref[rows, :] + _rms(o_ref[rows, :], half_gain)

        _for_row_chunks(h_ref.shape[0], post)


class _Cast(NamedTuple):
    src: jax.Array
    src_block: tuple
    src_map: Callable
    out_shape: tuple
    out_block: tuple
    out_map: Callable


def _cast_gate_up(w, layer, ni, nj):
    _, rows, cols = w.shape
    blk = (rows // ni, cols // nj)
    return _Cast(w, (None,) + blk, lambda i, j: (layer, i, j), (rows, cols), blk, lambda i, j: (i, j))


def _cast_down(w, layer, ni, nj):
    _, rows, cols = w.shape
    blk = (rows // nj, cols // ni)
    return _Cast(w, (None,) + blk, lambda i, j: (layer, j, i), (rows, cols), blk, lambda i, j: (j, i))


def _cast_all_layers(w, col_block, ni, nj):
    depth, rows, cols = w.shape
    n_col = cols // col_block
    assert cols % col_block == 0 and n_col <= nj and (depth * rows) % ni == 0
    blk = (depth * rows // ni, col_block)
    index = lambda i, j: (i, jnp.minimum(j, n_col - 1))
    return _Cast(w.reshape(depth * rows, cols), blk, index, (depth * rows, cols), blk, index)


def _ffn(x, gpre, w_gu, w_down, gpost, layer, casts=()):
    n, d = x.shape
    d_ff = w_down.shape[0]
    tm, tf = FFN_TOKEN_TILE, FFN_HIDDEN_TILE
    nj = d_ff // tf
    assert n % tm == 0 and d_ff % tf == 0
    cast_bytes = sum(2 * (4 + 2) * c.out_block[0] * c.out_block[1] for c in casts)
    vmem = (2 * 2 * tm * d * 4
            + tm * d * 2
            + 2 * 3 * d * tf * 2
            + 6 * FFN_ROW_SUBTILE * tf * 4
            + cast_bytes)
    outs = pl.pallas_call(
        functools.partial(_ffn_kernel, n_casts=len(casts)),
        grid=(n // tm, nj),
        in_specs=[
            pl.BlockSpec((tm, d), lambda i, j: (i, 0)),
            pl.BlockSpec((None, 1, d), lambda i, j: (layer, 0, 0)),
            pl.BlockSpec((d, tf), lambda i, j: (0, j)),
            pl.BlockSpec((d, tf), lambda i, j: (0, j + nj)),
            pl.BlockSpec((tf, d), lambda i, j: (j, 0)),
            pl.BlockSpec((None, 1, d), lambda i, j: (layer, 0, 0)),
        ] + [pl.BlockSpec(c.src_block, c.src_map) for c in casts],
        out_specs=[pl.BlockSpec((tm, d), lambda i, j: (i, 0))]
        + [pl.BlockSpec(c.out_block, c.out_map) for c in casts],
        out_shape=[jax.ShapeDtypeStruct((n, d), F32)]
        + [jax.ShapeDtypeStruct(c.out_shape, BF16) for c in casts],
        scratch_shapes=[pltpu.VMEM((tm, d), BF16)],
        compiler_params=pltpu.CompilerParams(
            dimension_semantics=("parallel", "arbitrary"),
            vmem_limit_bytes=min(vmem + (4 << 20), V7X_VMEM_BYTES - (4 << 20))),
        name="ffn",
    )(x, gpre, w_gu, w_gu, w_down, gpost, *[c.src for c in casts])
    return outs[0], outs[1:]


def _in_proj_kernel(x_ref, gpre_ref, w_ref, cw_ref, gc_ref, q_ref, k_ref, v_ref, cn_ref,
                    u_ref, *, tiles_per_seq):
    i = pl.program_id(0)
    tm = x_ref.shape[0]
    h = _rms(x_ref[...], gpre_ref[...]).astype(BF16)

    def proj(off, width):
        return jnp.dot(h, w_ref[:, off:off + width], preferred_element_type=F32)

    zq = proj(0, Q_DIM)
    for hd in range(N_Q_HEADS):
        q_ref[hd] = zq[:, hd * HEAD_DIM:(hd + 1) * HEAD_DIM]
    zk = proj(_OFF_K, KV_DIM)
    zv = proj(_OFF_V, KV_DIM)
    for hd in range(N_KV_HEADS):
        k_ref[hd] = zk[:, hd * HEAD_DIM:(hd + 1) * HEAD_DIM]
        v_ref[hd] = zv[:, hd * HEAD_DIM:(hd + 1) * HEAD_DIM]

    @pl.when(i % tiles_per_seq == 0)
    def _():
        u_ref[0:SUBLANES, :] = jnp.zeros((SUBLANES, CONV_DIM), F32)

    @pl.when(i % tiles_per_seq != 0)
    def _():
        u_ref[0:SUBLANES, :] = u_ref[tm:tm + SUBLANES, :]

    u_ref[SUBLANES:SUBLANES + tm, :] = proj(_OFF_C, CONV_DIM) * proj(_OFF_H, CONV_DIM)
    y = (cw_ref[2:3, :] * u_ref[SUBLANES:SUBLANES + tm, :]
         + cw_ref[1:2, :] * u_ref[SUBLANES - 1:SUBLANES - 1 + tm, :]
         + cw_ref[0:1, :] * u_ref[SUBLANES - 2:SUBLANES - 2 + tm, :])
    c = proj(_OFF_B, CONV_DIM) * y
    cn_ref[...] = _rms(c, gc_ref[...]).astype(BF16)


def _in_proj(x, gpre, w_in, conv_w, gconv, layer, seq, *, tm=512):
    n, d = x.shape
    assert n % tm == 0 and seq % tm == 0
    vmem = (2 * tm * d * 4 + 2 * d * IN_PROJ_DIM * 2
            + 2 * tm * (Q_DIM + 2 * KV_DIM) * 4 + 2 * tm * CONV_DIM * 2
            + (tm + SUBLANES) * CONV_DIM * 4 + 4 * tm * CONV_DIM * 4)
    return pl.pallas_call(
        functools.partial(_in_proj_kernel, tiles_per_seq=seq // tm),
        grid=(n // tm,),
        in_specs=[
            pl.BlockSpec((tm, d), lambda i: (i, 0)),
            pl.BlockSpec((None, 1, d), lambda i: (layer, 0, 0)),
            pl.BlockSpec((None, d, IN_PROJ_DIM), lambda i: (layer, 0, 0)),
            pl.BlockSpec((None, CONV_WIDTH, CONV_DIM), lambda i: (layer, 0, 0)),
            pl.BlockSpec((None, 1, CONV_DIM), lambda i: (layer, 0, 0)),
        ],
        out_specs=[
            pl.BlockSpec((N_Q_HEADS, tm, HEAD_DIM), lambda i: (0, i, 0)),
            pl.BlockSpec((N_KV_HEADS, tm, HEAD_DIM), lambda i: (0, i, 0)),
            pl.BlockSpec((N_KV_HEADS, tm, HEAD_DIM), lambda i: (0, i, 0)),
            pl.BlockSpec((tm, CONV_DIM), lambda i: (i, 0)),
        ],
        out_shape=[
            jax.ShapeDtypeStruct((N_Q_HEADS, n, HEAD_DIM), F32),
            jax.ShapeDtypeStruct((N_KV_HEADS, n, HEAD_DIM), F32),
            jax.ShapeDtypeStruct((N_KV_HEADS, n, HEAD_DIM), F32),
            jax.ShapeDtypeStruct((n, CONV_DIM), BF16),
        ],
        scratch_shapes=[pltpu.VMEM((tm + SUBLANES, CONV_DIM), F32)],
        compiler_params=pltpu.CompilerParams(
            dimension_semantics=("arbitrary",),
            vmem_limit_bytes=min(vmem + (4 << 20), V7X_VMEM_BYTES - (4 << 20))),
        name="in_proj",
    )(x, gpre, w_in, conv_w, gconv)


def _attn_kernel(q_ref, k_ref, v_ref, o_ref, acc_ref, m_ref, l_ref, tacc_ref, tm_ref, tl_ref):
    seq = k_ref.shape[0]
    logit_scale = (HEAD_DIM ** -0.5) * LOG2_E
    rows = Q_PER_KV * SPAN
    qi = lax.broadcasted_iota(jnp.int32, (rows, 2 * SPAN), 0) % SPAN
    kj = lax.broadcasted_iota(jnp.int32, (rows, 2 * SPAN), 1)
    cap_two = jnp.where((kj >= qi) & (kj <= qi + SPAN), jnp.inf, -jnp.inf)
    qi1 = lax.broadcasted_iota(jnp.int32, (rows, SPAN), 0) % SPAN
    kj1 = lax.broadcasted_iota(jnp.int32, (rows, SPAN), 1)
    cap_one = jnp.where(kj1 <= qi1, jnp.inf, -jnp.inf)

    def block(q_idx, k_idx, cap, src, dst):
        width = cap.shape[1]
        q = jnp.concatenate([q_ref[r, q_idx, :] for r in range(Q_PER_KV)], axis=0).astype(BF16)
        k = k_ref[k_idx, :].astype(BF16)
        v = v_ref[k_idx, :].astype(BF16)
        v_ones = jnp.concatenate([v, jnp.ones((width, HEAD_DIM), BF16)], axis=1)
        t = lax.dot_general(q, k, (((1,), (1,)), ((), ())), preferred_element_type=F32)
        t = jnp.minimum(t * logit_scale, cap)
        ps, m_news, a_olds = [], [], []
        for r in range(Q_PER_KV):
            t_r = t[r * SPAN:(r + 1) * SPAN]
            m_blk = jnp.broadcast_to(jnp.max(t_r, axis=-1, keepdims=True), (SPAN, HEAD_DIM))
            if src is None:
                m_new = m_blk
            else:
                m_old = src[1][r, src[3], :]
                m_new = jnp.maximum(m_old, m_blk)
                a_olds.append(jnp.exp2(m_old - m_new))
            m_news.append(m_new)
            p_r = [jnp.exp2(t_r[:, c * HEAD_DIM:(c + 1) * HEAD_DIM] - m_new)
                   for c in range(width // HEAD_DIM)]
            ps.append(jnp.concatenate(p_r, axis=1).astype(BF16))
        o = jnp.dot(jnp.concatenate(ps, axis=0), v_ones, preferred_element_type=F32)
        for r in range(Q_PER_KV):
            acc = o[r * SPAN:(r + 1) * SPAN, :HEAD_DIM]
            den = o[r * SPAN:(r + 1) * SPAN, HEAD_DIM:]
            if src is not None:
                acc = src[0][r, src[3], :] * a_olds[r] + acc
                den = src[2][r, src[3], :] * a_olds[r] + den
            if dst is None:
                o_ref[q_idx, r * HEAD_DIM:(r + 1) * HEAD_DIM] = acc / den
            else:
                dst[0][r, dst[3], :] = acc
                dst[1][r, dst[3], :] = m_news[r]
                dst[2][r, dst[3], :] = den

    near, mid, far = DILATIONS
    quarter = seq // mid
    nb_mid = quarter // SPAN
    class_state = (tacc_ref, tm_ref, tl_ref)
    seq_state = (acc_ref, m_ref, l_ref)

    def far_block(e, carry):
        idx = pl.ds(e, SPAN, stride=far)
        dst_idx = pl.ds((e % mid) * quarter + e // mid, SPAN, stride=mid)
        block(idx, idx, cap_one, None, class_state + (dst_idx,))
        return carry

    lax.fori_loop(0, far, far_block, 0, unroll=ATTN_UNROLL)

    def mid_first(e4, carry):
        idx = pl.ds(e4, SPAN, stride=mid)
        src_idx = pl.ds(pl.multiple_of(e4 * quarter, SPAN), SPAN)
        block(idx, idx, cap_one, class_state + (src_idx,), seq_state + (idx,))
        return carry

    lax.fori_loop(0, mid, mid_first, 0, unroll=ATTN_UNROLL)

    def mid_later(t, carry):
        e4 = t // (nb_mid - 1)
        nblk = t % (nb_mid - 1) + 1
        start = e4 + nblk * (mid * SPAN)
        q_idx = pl.ds(start, SPAN, stride=mid)
        k_idx = pl.ds(start - mid * SPAN, 2 * SPAN, stride=mid)
        src_idx = pl.ds(pl.multiple_of(e4 * quarter + nblk * SPAN, SPAN), SPAN)
        block(q_idx, k_idx, cap_two, class_state + (src_idx,), seq_state + (q_idx,))
        return carry

    lax.fori_loop(0, mid * (nb_mid - 1), mid_later, 0, unroll=ATTN_UNROLL)

    first = pl.ds(0, SPAN)
    block(first, first, cap_one, seq_state + (first,), None)

    def near_later(nblk, carry):
        q_idx = pl.ds(pl.multiple_of(nblk * SPAN, SPAN), SPAN)
        k_idx = pl.ds(pl.multiple_of((nblk - 1) * SPAN, SPAN), 2 * SPAN)
        block(q_idx, k_idx, cap_two, seq_state + (q_idx,), None)
        return carry

    lax.fori_loop(1, seq // (near * SPAN), near_later, 0, unroll=ATTN_UNROLL)


def _attention(q, k, v, batch, seq):
    n = q.shape[1]
    assert seq % (DILATIONS[2] * SPAN) == 0
    slab = Q_PER_KV * seq * HEAD_DIM * 4
    vmem = 2 * slab + 2 * 2 * seq * HEAD_DIM * 4 + 2 * slab + 6 * slab
    return pl.pallas_call(
        _attn_kernel,
        grid=(batch, N_KV_HEADS),
        in_specs=[
            pl.BlockSpec((Q_PER_KV, seq, HEAD_DIM), lambda b, g: (g, b, 0)),
            pl.BlockSpec((None, seq, HEAD_DIM), lambda b, g: (g, b, 0)),
            pl.BlockSpec((None, seq, HEAD_DIM), lambda b, g: (g, b, 0)),
        ],
        out_specs=pl.BlockSpec((seq, Q_PER_KV * HEAD_DIM), lambda b, g: (b, g)),
        out_shape=jax.ShapeDtypeStruct((n, Q_DIM), F32),
        scratch_shapes=[pltpu.VMEM((Q_PER_KV, seq, HEAD_DIM), F32)] * 6,
        compiler_params=pltpu.CompilerParams(
            dimension_semantics=("parallel", "parallel"),
            vmem_limit_bytes=min(vmem + (8 << 20), V7X_VMEM_BYTES - (4 << 20))),
        name="attn",
    )(q, k, v)


def _out_proj_kernel(a_ref, cn_ref, x_ref, ga_ref, w_ref, gpost_ref, o_ref):
    an = _rms(a_ref[...], ga_ref[...]).astype(BF16)
    mixed = jnp.dot(an, w_ref[0:Q_DIM, :], preferred_element_type=F32)
    mixed += jnp.dot(cn_ref[...], w_ref[Q_DIM:, :], preferred_element_type=F32)
    o_ref[...] = x_ref[...] + _rms(mixed, gpost_ref[...])


def _out_proj(a, cn, x, ga, w_out, gpost, layer, *, tm=512):
    n, d = x.shape
    assert n % tm == 0
    vmem = (2 * tm * Q_DIM * 4 + 2 * tm * CONV_DIM * 2 + 4 * tm * d * 4
            + 2 * (Q_DIM + CONV_DIM) * d * 2 + 2 * tm * d * 4)
    return pl.pallas_call(
        _out_proj_kernel,
        grid=(n // tm,),
        in_specs=[
            pl.BlockSpec((tm, Q_DIM), lambda i: (i, 0)),
            pl.BlockSpec((tm, CONV_DIM), lambda i: (i, 0)),
            pl.BlockSpec((tm, d), lambda i: (i, 0)),
            pl.BlockSpec((None, 1, Q_DIM), lambda i: (layer, 0, 0)),
            pl.BlockSpec((None, Q_DIM + CONV_DIM, d), lambda i: (layer, 0, 0)),
            pl.BlockSpec((None, 1, d), lambda i: (layer, 0, 0)),
        ],
        out_specs=pl.BlockSpec((tm, d), lambda i: (i, 0)),
        out_shape=jax.ShapeDtypeStruct((n, d), F32),
        compiler_params=pltpu.CompilerParams(
            dimension_semantics=("parallel",),
            vmem_limit_bytes=min(vmem + (4 << 20), V7X_VMEM_BYTES - (4 << 20))),
        name="out_proj",
    )(a, cn, x, ga, w_out, gpost)


def kernel(x, ffn1_norm_pre, ffn1_w_gate_up, ffn1_w_down, ffn1_norm_post, mix_norm_pre, w_in, conv_w, attn_out_norm, conv_out_norm, w_out, mix_norm_post, ffn2_norm_pre, ffn2_w_gate_up, ffn2_w_down, ffn2_norm_post):
    batch, seq, d = x.shape
    depth = w_in.shape[0]
    xf = x.reshape(batch * seq, d)

    def row(gain):
        return gain.reshape(gain.shape[0], 1, gain.shape[1])

    ni, nj = (batch * seq) // FFN_TOKEN_TILE, ffn1_w_down.shape[1] // FFN_HIDDEN_TILE
    w1 = (ffn1_w_gate_up[0].astype(BF16), ffn1_w_down[0].astype(BF16))
    for l in range(depth):
        casts = [_cast_gate_up(ffn2_w_gate_up, l, ni, nj), _cast_down(ffn2_w_down, l, ni, nj)]
        if l == 0:
            casts += [_cast_all_layers(w_in, IN_PROJ_CAST_COLS, ni, nj),
                      _cast_all_layers(w_out, OUT_PROJ_CAST_COLS, ni, nj)]
        xf, done = _ffn(xf, row(ffn1_norm_pre), w1[0], w1[1], row(ffn1_norm_post), l, casts)
        w2 = done[:2]
        if l == 0:
            w_in_b = done[2].reshape(w_in.shape)
            w_out_b = done[3].reshape(w_out.shape)
        q, k, v, cn = _in_proj(xf, row(mix_norm_pre), w_in_b, conv_w, row(conv_out_norm), l, seq)
        a = _attention(q, k, v, batch, seq)
        xf = _out_proj(a, cn, xf, row(attn_out_norm), w_out_b, row(mix_norm_post), l)
        casts = []
        if l + 1 < depth:
            casts = [_cast_gate_up(ffn1_w_gate_up, l + 1, ni, nj), _cast_down(ffn1_w_down, l + 1, ni, nj)]
        xf, w1 = _ffn(xf, row(ffn2_norm_pre), w2[0], w2[1], row(ffn2_norm_post), l, casts)
    return xf.reshape(batch, seq, d)
```

```python
import functools
from typing import Callable, NamedTuple

import jax
import jax.numpy as jnp
from jax import lax
from jax.experimental import pallas as pl
from jax.experimental.pallas import tpu as pltpu

D_MODEL = 2048
HEAD_DIM = 128
ATTN_WIDTH = D_MODEL // 2
CONV_DIM = D_MODEL - ATTN_WIDTH
N_Q_HEADS = ATTN_WIDTH // HEAD_DIM
N_KV_HEADS = max(1, N_Q_HEADS // 4)
Q_PER_KV = N_Q_HEADS // N_KV_HEADS
Q_DIM = N_Q_HEADS * HEAD_DIM
KV_DIM = N_KV_HEADS * HEAD_DIM
CONV_WIDTH = 3
IN_PROJ_DIM = Q_DIM + 2 * KV_DIM + 3 * CONV_DIM
SPAN = 128
DILATIONS = (1, 4, 16)
FFN_RESIDUAL_WEIGHT = 0.5
NORM_EPS = 1e-6
LOG2_E = 1.4426950408889634

V7X_VMEM_BYTES = 64 * 1024 * 1024
SUBLANES = 8
FFN_TOKEN_TILE = 1024
FFN_HIDDEN_TILE = 512
FFN_ROW_SUBTILE = 512
PROJ_ROW_SUBTILE = 256
IN_PROJ_CAST_COLS = 512
OUT_PROJ_CAST_COLS = 256
NORM_ROW_CHUNK = 128
ATTN_UNROLL = 4

BF16 = jnp.bfloat16
F32 = jnp.float32

_OFF_K = Q_DIM
_OFF_V = Q_DIM + KV_DIM
_OFF_H = Q_DIM + 2 * KV_DIM
_OFF_B = _OFF_H + CONV_DIM
_OFF_C = _OFF_B + CONV_DIM


def _rms(x, gain):
    y = x * lax.rsqrt(jnp.mean(x * x, axis=-1, keepdims=True) + NORM_EPS)
    return y * gain


def _ffn_kernel(*refs, n_casts):
    x_ref, gpre_ref, wg_ref, wu_ref, wd_ref, gpost_ref = refs[:6]
    cast_src = refs[6:6 + n_casts]
    o_ref = refs[6 + n_casts]
    cast_dst = refs[7 + n_casts:7 + 2 * n_casts]
    h_ref = refs[7 + 2 * n_casts]
    j = pl.program_id(1)
    last = pl.num_programs(1) - 1
    n_rows = h_ref.shape[0]

    def step(first_step, last_step):
        for src, dst in zip(cast_src, cast_dst):
            dst[...] = src[...].astype(BF16)
        if last_step:
            half_gain = FFN_RESIDUAL_WEIGHT * gpost_ref[...]
        for r0 in range(0, n_rows, FFN_ROW_SUBTILE):
            if first_step:
                for c0 in range(r0, r0 + FFN_ROW_SUBTILE, NORM_ROW_CHUNK):
                    rows = slice(c0, c0 + NORM_ROW_CHUNK)
                    h_ref[rows, :] = _rms(x_ref[rows, :], gpre_ref[...]).astype(BF16)
            rows = slice(r0, r0 + FFN_ROW_SUBTILE)
            h = h_ref[rows, :]
            g = jnp.dot(h, wg_ref[...], preferred_element_type=F32)
            u = jnp.dot(h, wu_ref[...], preferred_element_type=F32)
            a = (g * jax.nn.sigmoid(g) * u).astype(BF16)
            part = jnp.dot(a, wd_ref[...], preferred_element_type=F32)
            if first_step:
                o_ref[rows, :] = part
            else:
                o_ref[rows, :] += part
            if last_step:
                for c0 in range(r0, r0 + FFN_ROW_SUBTILE, NORM_ROW_CHUNK):
                    rows = slice(c0, c0 + NORM_ROW_CHUNK)
                    o_ref[rows, :] = x_ref[rows, :] + _rms(o_ref[rows, :], half_gain)

    pl.when(j == 0)(lambda: step(True, False))
    pl.when((j > 0) & (j < last))(lambda: step(False, False))
    pl.when(j == last)(lambda: step(False, True))


class _Cast(NamedTuple):
    src: jax.Array
    src_block: tuple
    src_map: Callable
    out_shape: tuple
    out_block: tuple
    out_map: Callable


def _cast_gate_up(w, layer, ni, nj):
    _, rows, cols = w.shape
    blk = (rows // ni, cols // nj)
    return _Cast(w, (None,) + blk, lambda i, j: (layer, i, j), (rows, cols), blk, lambda i, j: (i, j))


def _cast_down(w, layer, ni, nj):
    _, rows, cols = w.shape
    blk = (rows // nj, cols // ni)
    return _Cast(w, (None,) + blk, lambda i, j: (layer, j, i), (rows, cols), blk, lambda i, j: (j, i))


def _cast_all_layers(w, col_block, ni, nj):
    depth, rows, cols = w.shape
    n_col = cols // col_block
    assert cols % col_block == 0 and n_col <= nj and (depth * rows) % ni == 0
    blk = (depth * rows // ni, col_block)
    index = lambda i, j: (i, jnp.minimum(j, n_col - 1))
    return _Cast(w.reshape(depth * rows, cols), blk, index, (depth * rows, cols), blk, index)


def _ffn(x, gpre, w_gu, w_down, gpost, layer, casts=()):
    n, d = x.shape
    d_ff = w_down.shape[0]
    tm, tf = FFN_TOKEN_TILE, FFN_HIDDEN_TILE
    nj = d_ff // tf
    assert n % tm == 0 and d_ff % tf == 0
    cast_bytes = sum(2 * (4 + 2) * c.out_block[0] * c.out_block[1] for c in casts)
    vmem = (2 * 2 * tm * d * 4
            + tm * d * 2
            + 2 * 3 * d * tf * 2
            + 6 * FFN_ROW_SUBTILE * tf * 4
            + cast_bytes)
    outs = pl.pallas_call(
        functools.partial(_ffn_kernel, n_casts=len(casts)),
        grid=(n // tm, nj),
        in_specs=[
            pl.BlockSpec((tm, d), lambda i, j: (i, 0)),
            pl.BlockSpec((None, 1, d), lambda i, j: (layer, 0, 0)),
            pl.BlockSpec((d, tf), lambda i, j: (0, j)),
            pl.BlockSpec((d, tf), lambda i, j: (0, j + nj)),
            pl.BlockSpec((tf, d), lambda i, j: (j, 0)),
            pl.BlockSpec((None, 1, d), lambda i, j: (layer, 0, 0)),
        ] + [pl.BlockSpec(c.src_block, c.src_map) for c in casts],
        out_specs=[pl.BlockSpec((tm, d), lambda i, j: (i, 0))]
        + [pl.BlockSpec(c.out_block, c.out_map) for c in casts],
        out_shape=[jax.ShapeDtypeStruct((n, d), F32)]
        + [jax.ShapeDtypeStruct(c.out_shape, BF16) for c in casts],
        scratch_shapes=[pltpu.VMEM((tm, d), BF16)],
        compiler_params=pltpu.CompilerParams(
            dimension_semantics=("parallel", "arbitrary"),
            vmem_limit_bytes=min(vmem + (4 << 20), V7X_VMEM_BYTES - (4 << 20))),
        name="ffn",
    )(x, gpre, w_gu, w_gu, w_down, gpost, *[c.src for c in casts])
    return outs[0], outs[1:]


def _in_proj_kernel(x_ref, gpre_ref, w_ref, cw_ref, gc_ref, q_ref, k_ref, v_ref, cn_ref,
                    u_ref, *, tiles_per_seq):
    i = pl.program_id(0)
    tm = x_ref.shape[0]

    @pl.when(i % tiles_per_seq == 0)
    def _():
        u_ref[0:SUBLANES, :] = jnp.zeros((SUBLANES, CONV_DIM), F32)

    @pl.when(i % tiles_per_seq != 0)
    def _():
        u_ref[0:SUBLANES, :] = u_ref[tm:tm + SUBLANES, :]

    for r0 in range(0, tm, PROJ_ROW_SUBTILE):
        rows = slice(r0, r0 + PROJ_ROW_SUBTILE)
        h = _rms(x_ref[rows, :], gpre_ref[...]).astype(BF16)

        def proj(off, width):
            return jnp.dot(h, w_ref[:, off:off + width], preferred_element_type=F32)

        zq = proj(0, Q_DIM)
        for hd in range(N_Q_HEADS):
            q_ref[hd, rows, :] = zq[:, hd * HEAD_DIM:(hd + 1) * HEAD_DIM]
        zk = proj(_OFF_K, KV_DIM)
        zv = proj(_OFF_V, KV_DIM)
        for hd in range(N_KV_HEADS):
            k_ref[hd, rows, :] = zk[:, hd * HEAD_DIM:(hd + 1) * HEAD_DIM]
            v_ref[hd, rows, :] = zv[:, hd * HEAD_DIM:(hd + 1) * HEAD_DIM]

        base = SUBLANES + r0
        u_ref[base:base + PROJ_ROW_SUBTILE, :] = proj(_OFF_C, CONV_DIM) * proj(_OFF_H, CONV_DIM)
        y = (cw_ref[2:3, :] * u_ref[base:base + PROJ_ROW_SUBTILE, :]
             + cw_ref[1:2, :] * u_ref[base - 1:base - 1 + PROJ_ROW_SUBTILE, :]
             + cw_ref[0:1, :] * u_ref[base - 2:base - 2 + PROJ_ROW_SUBTILE, :])
        c = proj(_OFF_B, CONV_DIM) * y
        cn_ref[rows, :] = _rms(c, gc_ref[...]).astype(BF16)


def _in_proj(x, gpre, w_in, conv_w, gconv, layer, seq, *, tm=512):
    n, d = x.shape
    assert n % tm == 0 and seq % tm == 0
    vmem = (2 * tm * d * 4 + 2 * d * IN_PROJ_DIM * 2
            + 2 * tm * (Q_DIM + 2 * KV_DIM) * 4 + 2 * tm * CONV_DIM * 2
            + (tm + SUBLANES) * CONV_DIM * 4 + 4 * tm * CONV_DIM * 4)
    return pl.pallas_call(
        functools.partial(_in_proj_kernel, tiles_per_seq=seq // tm),
        grid=(n // tm,),
        in_specs=[
            pl.BlockSpec((tm, d), lambda i: (i, 0)),
            pl.BlockSpec((None, 1, d), lambda i: (layer, 0, 0)),
            pl.BlockSpec((None, d, IN_PROJ_DIM), lambda i: (layer, 0, 0)),
            pl.BlockSpec((None, CONV_WIDTH, CONV_DIM), lambda i: (layer, 0, 0)),
            pl.BlockSpec((None, 1, CONV_DIM), lambda i: (layer, 0, 0)),
        ],
        out_specs=[
            pl.BlockSpec((N_Q_HEADS, tm, HEAD_DIM), lambda i: (0, i, 0)),
            pl.BlockSpec((N_KV_HEADS, tm, HEAD_DIM), lambda i: (0, i, 0)),
            pl.BlockSpec((N_KV_HEADS, tm, HEAD_DIM), lambda i: (0, i, 0)),
            pl.BlockSpec((tm, CONV_DIM), lambda i: (i, 0)),
        ],
        out_shape=[
            jax.ShapeDtypeStruct((N_Q_HEADS, n, HEAD_DIM), F32),
            jax.ShapeDtypeStruct((N_KV_HEADS, n, HEAD_DIM), F32),
            jax.ShapeDtypeStruct((N_KV_HEADS, n, HEAD_DIM), F32),
            jax.ShapeDtypeStruct((n, CONV_DIM), BF16),
        ],
        scratch_shapes=[pltpu.VMEM((tm + SUBLANES, CONV_DIM), F32)],
        compiler_params=pltpu.CompilerParams(
            dimension_semantics=("arbitrary",),
            vmem_limit_bytes=min(vmem + (4 << 20), V7X_VMEM_BYTES - (4 << 20))),
        name="in_proj",
    )(x, gpre, w_in, conv_w, gconv)


def _attn_kernel(q_ref, k_ref, v_ref, o_ref, acc_ref, m_ref, l_ref, tacc_ref, tm_ref, tl_ref):
    seq = k_ref.shape[0]
    logit_scale = (HEAD_DIM ** -0.5) * LOG2_E
    rows = Q_PER_KV * SPAN
    qi = lax.broadcasted_iota(jnp.int32, (rows, 2 * SPAN), 0) % SPAN
    kj = lax.broadcasted_iota(jnp.int32, (rows, 2 * SPAN), 1)
    cap_two = jnp.where((kj >= qi) & (kj <= qi + SPAN), jnp.inf, -jnp.inf)
    qi1 = lax.broadcasted_iota(jnp.int32, (rows, SPAN), 0) % SPAN
    kj1 = lax.broadcasted_iota(jnp.int32, (rows, SPAN), 1)
    cap_one = jnp.where(kj1 <= qi1, jnp.inf, -jnp.inf)

    def block(q_idx, k_idx, cap, src, dst):
        width = cap.shape[1]
        q = jnp.concatenate([q_ref[r, q_idx, :] for r in range(Q_PER_KV)], axis=0).astype(BF16)
        k = k_ref[k_idx, :].astype(BF16)
        v = v_ref[k_idx, :].astype(BF16)
        v_ones = jnp.concatenate([v, jnp.ones((width, HEAD_DIM), BF16)], axis=1)
        t = lax.dot_general(q, k, (((1,), (1,)), ((), ())), preferred_element_type=F32)
        t = jnp.minimum(t * logit_scale, cap)
        ps, m_news, a_olds = [], [], []
        for r in range(Q_PER_KV):
            t_r = t[r * SPAN:(r + 1) * SPAN]
            m_blk = jnp.broadcast_to(jnp.max(t_r, axis=-1, keepdims=True), (SPAN, HEAD_DIM))
            if src is None:
                m_new = m_blk
            else:
                m_old = src[1][r, src[3], :]
                m_new = jnp.maximum(m_old, m_blk)
                a_olds.append(jnp.exp2(m_old - m_new))
            m_news.append(m_new)
            p_r = [jnp.exp2(t_r[:, c * HEAD_DIM:(c + 1) * HEAD_DIM] - m_new)
                   for c in range(width // HEAD_DIM)]
            ps.append(jnp.concatenate(p_r, axis=1).astype(BF16))
        o = jnp.dot(jnp.concatenate(ps, axis=0), v_ones, preferred_element_type=F32)
        for r in range(Q_PER_KV):
            acc = o[r * SPAN:(r + 1) * SPAN, :HEAD_DIM]
            den = o[r * SPAN:(r + 1) * SPAN, HEAD_DIM:]
            if src is not None:
                acc = src[0][r, src[3], :] * a_olds[r] + acc
                den = src[2][r, src[3], :] * a_olds[r] + den
            if dst is None:
                o_ref[q_idx, r * HEAD_DIM:(r + 1) * HEAD_DIM] = acc / den
            else:
                dst[0][r, dst[3], :] = acc
                dst[1][r, dst[3], :] = m_news[r]
                dst[2][r, dst[3], :] = den

    near, mid, far = DILATIONS
    quarter = seq // mid
    nb_mid = quarter // SPAN
    class_state = (tacc_ref, tm_ref, tl_ref)
    seq_state = (acc_ref, m_ref, l_ref)

    def far_block(e, carry):
        idx = pl.ds(e, SPAN, stride=far)
        dst_idx = pl.ds((e % mid) * quarter + e // mid, SPAN, stride=mid)
        block(idx, idx, cap_one, None, class_state + (dst_idx,))
        return carry

    lax.fori_loop(0, far, far_block, 0, unroll=ATTN_UNROLL)

    def mid_first(e4, carry):
        idx = pl.ds(e4, SPAN, stride=mid)
        src_idx = pl.ds(pl.multiple_of(e4 * quarter, SPAN), SPAN)
        block(idx, idx, cap_one, class_state + (src_idx,), seq_state + (idx,))
        return carry

    lax.fori_loop(0, mid, mid_first, 0, unroll=ATTN_UNROLL)

    def mid_later(t, carry):
        e4 = t // (nb_mid - 1)
        nblk = t % (nb_mid - 1) + 1
        start = e4 + nblk * (mid * SPAN)
        q_idx = pl.ds(start, SPAN, stride=mid)
        k_idx = pl.ds(start - mid * SPAN, 2 * SPAN, stride=mid)
        src_idx = pl.ds(pl.multiple_of(e4 * quarter + nblk * SPAN, SPAN), SPAN)
        block(q_idx, k_idx, cap_two, class_state + (src_idx,), seq_state + (q_idx,))
        return carry

    lax.fori_loop(0, mid * (nb_mid - 1), mid_later, 0, unroll=ATTN_UNROLL)

    first = pl.ds(0, SPAN)
    block(first, first, cap_one, seq_state + (first,), None)

    def near_later(nblk, carry):
        q_idx = pl.ds(pl.multiple_of(nblk * SPAN, SPAN), SPAN)
        k_idx = pl.ds(pl.multiple_of((nblk - 1) * SPAN, SPAN), 2 * SPAN)
        block(q_idx, k_idx, cap_two, seq_state + (q_idx,), None)
        return carry

    lax.fori_loop(1, seq // (near * SPAN), near_later, 0, unroll=ATTN_UNROLL)


def _attention(q, k, v, batch, seq):
    n = q.shape[1]
    assert seq % (DILATIONS[2] * SPAN) == 0
    slab = Q_PER_KV * seq * HEAD_DIM * 4
    vmem = 2 * slab + 2 * 2 * seq * HEAD_DIM * 4 + 2 * slab + 6 * slab
    return pl.pallas_call(
        _attn_kernel,
        grid=(batch, N_KV_HEADS),
        in_specs=[
            pl.BlockSpec((Q_PER_KV, seq, HEAD_DIM), lambda b, g: (g, b, 0)),
            pl.BlockSpec((None, seq, HEAD_DIM), lambda b, g: (g, b, 0)),
            pl.BlockSpec((None, seq, HEAD_DIM), lambda b, g: (g, b, 0)),
        ],
        out_specs=pl.BlockSpec((seq, Q_PER_KV * HEAD_DIM), lambda b, g: (b, g)),
        out_shape=jax.ShapeDtypeStruct((n, Q_DIM), F32),
        scratch_shapes=[pltpu.VMEM((Q_PER_KV, seq, HEAD_DIM), F32)] * 6,
        compiler_params=pltpu.CompilerParams(
            dimension_semantics=("parallel", "parallel"),
            vmem_limit_bytes=min(vmem + (8 << 20), V7X_VMEM_BYTES - (4 << 20))),
        name="attn",
    )(q, k, v)


def _out_proj_kernel(a_ref, cn_ref, x_ref, ga_ref, w_ref, gpost_ref, o_ref):
    an = _rms(a_ref[...], ga_ref[...]).astype(BF16)
    mixed = jnp.dot(an, w_ref[0:Q_DIM, :], preferred_element_type=F32)
    mixed += jnp.dot(cn_ref[...], w_ref[Q_DIM:, :], preferred_element_type=F32)
    o_ref[...] = x_ref[...] + _rms(mixed, gpost_ref[...])


def _out_proj(a, cn, x, ga, w_out, gpost, layer, *, tm=512):
    n, d = x.shape
    assert n % tm == 0
    vmem = (2 * tm * Q_DIM * 4 + 2 * tm * CONV_DIM * 2 + 4 * tm * d * 4
            + 2 * (Q_DIM + CONV_DIM) * d * 2 + 2 * tm * d * 4)
    return pl.pallas_call(
        _out_proj_kernel,
        grid=(n // tm,),
        in_specs=[
            pl.BlockSpec((tm, Q_DIM), lambda i: (i, 0)),
            pl.BlockSpec((tm, CONV_DIM), lambda i: (i, 0)),
            pl.BlockSpec((tm, d), lambda i: (i, 0)),
            pl.BlockSpec((None, 1, Q_DIM), lambda i: (layer, 0, 0)),
            pl.BlockSpec((None, Q_DIM + CONV_DIM, d), lambda i: (layer, 0, 0)),
            pl.BlockSpec((None, 1, d), lambda i: (layer, 0, 0)),
        ],
        out_specs=pl.BlockSpec((tm, d), lambda i: (i, 0)),
        out_shape=jax.ShapeDtypeStruct((n, d), F32),
        compiler_params=pltpu.CompilerParams(
            dimension_semantics=("parallel",),
            vmem_limit_bytes=min(vmem + (4 << 20), V7X_VMEM_BYTES - (4 << 20))),
        name="out_proj",
    )(a, cn, x, ga, w_out, gpost)


def kernel(x, ffn1_norm_pre, ffn1_w_gate_up, ffn1_w_down, ffn1_norm_post, mix_norm_pre, w_in, conv_w, attn_out_norm, conv_out_norm, w_out, mix_norm_post, ffn2_norm_pre, ffn2_w_gate_up, ffn2_w_down, ffn2_norm_post):
    batch, seq, d = x.shape
    depth = w_in.shape[0]
    xf = x.reshape(batch * seq, d)

    def row(gain):
        return gain.reshape(gain.shape[0], 1, gain.shape[1])

    ni, nj = (batch * seq) // FFN_TOKEN_TILE, ffn1_w_down.shape[1] // FFN_HIDDEN_TILE
    w1 = (ffn1_w_gate_up[0].astype(BF16), ffn1_w_down[0].astype(BF16))
    for l in range(depth):
        casts = [_cast_gate_up(ffn2_w_gate_up, l, ni, nj), _cast_down(ffn2_w_down, l, ni, nj)]
        if l == 0:
            casts += [_cast_all_layers(w_in, IN_PROJ_CAST_COLS, ni, nj),
                      _cast_all_layers(w_out, OUT_PROJ_CAST_COLS, ni, nj)]
        xf, done = _ffn(xf, row(ffn1_norm_pre), w1[0], w1[1], row(ffn1_norm_post), l, casts)
        w2 = done[:2]
        if l == 0:
            w_in_b = done[2].reshape(w_in.shape)
            w_out_b = done[3].reshape(w_out.shape)
        q, k, v, cn = _in_proj(xf, row(mix_norm_pre), w_in_b, conv_w, row(conv_out_norm), l, seq)
        a = _attention(q, k, v, batch, seq)
        xf = _out_proj(a, cn, xf, row(attn_out_norm), w_out_b, row(mix_norm_post), l)
        casts = []
        if l + 1 < depth:
            casts = [_cast_gate_up(ffn1_w_gate_up, l + 1, ni, nj), _cast_down(ffn1_w_down, l + 1, ni, nj)]
        xf, w1 = _ffn(xf, row(ffn2_norm_pre), w2[0], w2[1], row(ffn2_norm_post), l, casts)
    return xf.reshape(batch, seq, d)
```

```python
import functools
from typing import Callable, NamedTuple

import jax
import jax.numpy as jnp
from jax import lax
from jax.experimental import pallas as pl
from jax.experimental.pallas import tpu as pltpu

D_MODEL = 2048
HEAD_DIM = 128
ATTN_WIDTH = D_MODEL // 2
CONV_DIM = D_MODEL - ATTN_WIDTH
N_Q_HEADS = ATTN_WIDTH // HEAD_DIM
N_KV_HEADS = max(1, N_Q_HEADS // 4)
Q_PER_KV = N_Q_HEADS // N_KV_HEADS
Q_DIM = N_Q_HEADS * HEAD_DIM
KV_DIM = N_KV_HEADS * HEAD_DIM
CONV_WIDTH = 3
IN_PROJ_DIM = Q_DIM + 2 * KV_DIM + 3 * CONV_DIM
SPAN = 128
DILATIONS = (1, 4, 16)
FFN_RESIDUAL_WEIGHT = 0.5
NORM_EPS = 1e-6
LOG2_E = 1.4426950408889634

V7X_VMEM_BYTES = 64 * 1024 * 1024
SUBLANES = 8
FFN_TOKEN_TILE = 1024
FFN_HIDDEN_TILE = 512
FFN_ROW_SUBTILE = 512
PROJ_ROW_SUBTILE = 256
IN_PROJ_CAST_COLS = 512
OUT_PROJ_CAST_COLS = 256
NORM_ROW_CHUNK = 128
ATTN_UNROLL = 8

BF16 = jnp.bfloat16
F32 = jnp.float32

_OFF_K = Q_DIM
_OFF_V = Q_DIM + KV_DIM
_OFF_H = Q_DIM + 2 * KV_DIM
_OFF_B = _OFF_H + CONV_DIM
_OFF_C = _OFF_B + CONV_DIM


def _rms(x, gain):
    y = x * lax.rsqrt(jnp.mean(x * x, axis=-1, keepdims=True) + NORM_EPS)
    return y * gain


def _ffn_kernel(*refs, n_casts):
    x_ref, gpre_ref, wg_ref, wu_ref, wd_ref, gpost_ref = refs[:6]
    cast_src = refs[6:6 + n_casts]
    o_ref = refs[6 + n_casts]
    cast_dst = refs[7 + n_casts:7 + 2 * n_casts]
    h_ref = refs[7 + 2 * n_casts]
    j = pl.program_id(1)
    last = pl.num_programs(1) - 1
    n_rows = h_ref.shape[0]

    def step(first_step, last_step):
        for src, dst in zip(cast_src, cast_dst):
            dst[...] = src[...].astype(BF16)
        if last_step:
            half_gain = FFN_RESIDUAL_WEIGHT * gpost_ref[...]
        for r0 in range(0, n_rows, FFN_ROW_SUBTILE):
            if first_step:
                for c0 in range(r0, r0 + FFN_ROW_SUBTILE, NORM_ROW_CHUNK):
                    rows = slice(c0, c0 + NORM_ROW_CHUNK)
                    h_ref[rows, :] = _rms(x_ref[rows, :], gpre_ref[...]).astype(BF16)
            rows = slice(r0, r0 + FFN_ROW_SUBTILE)
            h = h_ref[rows, :]
            g = jnp.dot(h, wg_ref[...], preferred_element_type=F32)
            u = jnp.dot(h, wu_ref[...], preferred_element_type=F32)
            a = (g * jax.nn.sigmoid(g) * u).astype(BF16)
            part = jnp.dot(a, wd_ref[...], preferred_element_type=F32)
            if first_step:
                o_ref[rows, :] = part
            else:
                o_ref[rows, :] += part
            if last_step:
                for c0 in range(r0, r0 + FFN_ROW_SUBTILE, NORM_ROW_CHUNK):
                    rows = slice(c0, c0 + NORM_ROW_CHUNK)
                    o_ref[rows, :] = x_ref[rows, :] + _rms(o_ref[rows, :], half_gain)

    pl.when(j == 0)(lambda: step(True, False))
    pl.when((j > 0) & (j < last))(lambda: step(False, False))
    pl.when(j == last)(lambda: step(False, True))


class _Cast(NamedTuple):
    src: jax.Array
    src_block: tuple
    src_map: Callable
    out_shape: tuple
    out_block: tuple
    out_map: Callable


def _cast_gate_up(w, layer, ni, nj):
    _, rows, cols = w.shape
    blk = (rows // ni, cols // nj)
    return _Cast(w, (None,) + blk, lambda i, j: (layer, i, j), (rows, cols), blk, lambda i, j: (i, j))


def _cast_down(w, layer, ni, nj):
    _, rows, cols = w.shape
    blk = (rows // nj, cols // ni)
    return _Cast(w, (None,) + blk, lambda i, j: (layer, j, i), (rows, cols), blk, lambda i, j: (j, i))


def _cast_all_layers(w, col_block, ni, nj):
    depth, rows, cols = w.shape
    n_col = cols // col_block
    assert cols % col_block == 0 and n_col <= nj and (depth * rows) % ni == 0
    blk = (depth * rows // ni, col_block)
    index = lambda i, j: (i, jnp.minimum(j, n_col - 1))
    return _Cast(w.reshape(depth * rows, cols), blk, index, (depth * rows, cols), blk, index)


def _ffn(x, gpre, w_gu, w_down, gpost, layer, casts=()):
    n, d = x.shape
    d_ff = w_down.shape[0]
    tm, tf = FFN_TOKEN_TILE, FFN_HIDDEN_TILE
    nj = d_ff // tf
    assert n % tm == 0 and d_ff % tf == 0
    cast_bytes = sum(2 * (4 + 2) * c.out_block[0] * c.out_block[1] for c in casts)
    vmem = (2 * 2 * tm * d * 4
            + tm * d * 2
            + 2 * 3 * d * tf * 2
            + 6 * FFN_ROW_SUBTILE * tf * 4
            + cast_bytes)
    outs = pl.pallas_call(
        functools.partial(_ffn_kernel, n_casts=len(casts)),
        grid=(n // tm, nj),
        in_specs=[
            pl.BlockSpec((tm, d), lambda i, j: (i, 0)),
            pl.BlockSpec((None, 1, d), lambda i, j: (layer, 0, 0)),
            pl.BlockSpec((d, tf), lambda i, j: (0, j)),
            pl.BlockSpec((d, tf), lambda i, j: (0, j + nj)),
            pl.BlockSpec((tf, d), lambda i, j: (j, 0)),
            pl.BlockSpec((None, 1, d), lambda i, j: (layer, 0, 0)),
        ] + [pl.BlockSpec(c.src_block, c.src_map) for c in casts],
        out_specs=[pl.BlockSpec((tm, d), lambda i, j: (i, 0))]
        + [pl.BlockSpec(c.out_block, c.out_map) for c in casts],
        out_shape=[jax.ShapeDtypeStruct((n, d), F32)]
        + [jax.ShapeDtypeStruct(c.out_shape, BF16) for c in casts],
        scratch_shapes=[pltpu.VMEM((tm, d), BF16)],
        compiler_params=pltpu.CompilerParams(
            dimension_semantics=("parallel", "arbitrary"),
            vmem_limit_bytes=min(vmem + (4 << 20), V7X_VMEM_BYTES - (4 << 20))),
        name="ffn",
    )(x, gpre, w_gu, w_gu, w_down, gpost, *[c.src for c in casts])
    return outs[0], outs[1:]


def _in_proj_kernel(x_ref, gpre_ref, w_ref, cw_ref, gc_ref, q_ref, k_ref, v_ref, cn_ref,
                    u_ref, *, tiles_per_seq):
    i = pl.program_id(0)
    tm = x_ref.shape[0]

    @pl.when(i % tiles_per_seq == 0)
    def _():
        u_ref[0:SUBLANES, :] = jnp.zeros((SUBLANES, CONV_DIM), F32)

    @pl.when(i % tiles_per_seq != 0)
    def _():
        u_ref[0:SUBLANES, :] = u_ref[tm:tm + SUBLANES, :]

    for r0 in range(0, tm, PROJ_ROW_SUBTILE):
        rows = slice(r0, r0 + PROJ_ROW_SUBTILE)
        h = _rms(x_ref[rows, :], gpre_ref[...]).astype(BF16)

        def proj(off, width):
            return jnp.dot(h, w_ref[:, off:off + width], preferred_element_type=F32)

        zq = proj(0, Q_DIM)
        for hd in range(N_Q_HEADS):
            q_ref[hd, rows, :] = zq[:, hd * HEAD_DIM:(hd + 1) * HEAD_DIM]
        zk = proj(_OFF_K, KV_DIM)
        zv = proj(_OFF_V, KV_DIM)
        for hd in range(N_KV_HEADS):
            k_ref[hd, rows, :] = zk[:, hd * HEAD_DIM:(hd + 1) * HEAD_DIM]
            v_ref[hd, rows, :] = zv[:, hd * HEAD_DIM:(hd + 1) * HEAD_DIM]

        base = SUBLANES + r0
        u_ref[base:base + PROJ_ROW_SUBTILE, :] = proj(_OFF_C, CONV_DIM) * proj(_OFF_H, CONV_DIM)
        y = (cw_ref[2:3, :] * u_ref[base:base + PROJ_ROW_SUBTILE, :]
             + cw_ref[1:2, :] * u_ref[base - 1:base - 1 + PROJ_ROW_SUBTILE, :]
             + cw_ref[0:1, :] * u_ref[base - 2:base - 2 + PROJ_ROW_SUBTILE, :])
        c = proj(_OFF_B, CONV_DIM) * y
        cn_ref[rows, :] = _rms(c, gc_ref[...]).astype(BF16)


def _in_proj(x, gpre, w_in, conv_w, gconv, layer, seq, *, tm=512):
    n, d = x.shape
    assert n % tm == 0 and seq % tm == 0
    vmem = (2 * tm * d * 4 + 2 * d * IN_PROJ_DIM * 2
            + 2 * tm * (Q_DIM + 2 * KV_DIM) * 4 + 2 * tm * CONV_DIM * 2
            + (tm + SUBLANES) * CONV_DIM * 4 + 4 * tm * CONV_DIM * 4)
    return pl.pallas_call(
        functools.partial(_in_proj_kernel, tiles_per_seq=seq // tm),
        grid=(n // tm,),
        in_specs=[
            pl.BlockSpec((tm, d), lambda i: (i, 0)),
            pl.BlockSpec((None, 1, d), lambda i: (layer, 0, 0)),
            pl.BlockSpec((None, d, IN_PROJ_DIM), lambda i: (layer, 0, 0)),
            pl.BlockSpec((None, CONV_WIDTH, CONV_DIM), lambda i: (layer, 0, 0)),
            pl.BlockSpec((None, 1, CONV_DIM), lambda i: (layer, 0, 0)),
        ],
        out_specs=[
            pl.BlockSpec((N_Q_HEADS, tm, HEAD_DIM), lambda i: (0, i, 0)),
            pl.BlockSpec((N_KV_HEADS, tm, HEAD_DIM), lambda i: (0, i, 0)),
            pl.BlockSpec((N_KV_HEADS, tm, HEAD_DIM), lambda i: (0, i, 0)),
            pl.BlockSpec((tm, CONV_DIM), lambda i: (i, 0)),
        ],
        out_shape=[
            jax.ShapeDtypeStruct((N_Q_HEADS, n, HEAD_DIM), F32),
            jax.ShapeDtypeStruct((N_KV_HEADS, n, HEAD_DIM), F32),
            jax.ShapeDtypeStruct((N_KV_HEADS, n, HEAD_DIM), F32),
            jax.ShapeDtypeStruct((n, CONV_DIM), BF16),
        ],
        scratch_shapes=[pltpu.VMEM((tm + SUBLANES, CONV_DIM), F32)],
        compiler_params=pltpu.CompilerParams(
            dimension_semantics=("arbitrary",),
            vmem_limit_bytes=min(vmem + (4 << 20), V7X_VMEM_BYTES - (4 << 20))),
        name="in_proj",
    )(x, gpre, w_in, conv_w, gconv)


def _attn_kernel(q_ref, k_ref, v_ref, o_ref, acc_ref, m_ref, l_ref, tacc_ref, tm_ref, tl_ref):
    seq = k_ref.shape[0]
    logit_scale = (HEAD_DIM ** -0.5) * LOG2_E
    rows = Q_PER_KV * SPAN
    qi = lax.broadcasted_iota(jnp.int32, (rows, 2 * SPAN), 0) % SPAN
    kj = lax.broadcasted_iota(jnp.int32, (rows, 2 * SPAN), 1)
    cap_two = jnp.where((kj >= qi) & (kj <= qi + SPAN), jnp.inf, -jnp.inf)
    qi1 = lax.broadcasted_iota(jnp.int32, (rows, SPAN), 0) % SPAN
    kj1 = lax.broadcasted_iota(jnp.int32, (rows, SPAN), 1)
    cap_one = jnp.where(kj1 <= qi1, jnp.inf, -jnp.inf)

    def block(q_idx, k_idx, cap, src, dst):
        width = cap.shape[1]
        q = jnp.concatenate([q_ref[r, q_idx, :] for r in range(Q_PER_KV)], axis=0).astype(BF16)
        k = k_ref[k_idx, :].astype(BF16)
        v = v_ref[k_idx, :].astype(BF16)
        v_ones = jnp.concatenate([v, jnp.ones((width, HEAD_DIM), BF16)], axis=1)
        t = lax.dot_general(q, k, (((1,), (1,)), ((), ())), preferred_element_type=F32)
        t = jnp.minimum(t * logit_scale, cap)
        ps, m_news, a_olds = [], [], []
        for r in range(Q_PER_KV):
            t_r = t[r * SPAN:(r + 1) * SPAN]
            m_blk = jnp.broadcast_to(jnp.max(t_r, axis=-1, keepdims=True), (SPAN, HEAD_DIM))
            if src is None:
                m_new = m_blk
            else:
                m_old = src[1][r, src[3], :]
                m_new = jnp.maximum(m_old, m_blk)
                a_olds.append(jnp.exp2(m_old - m_new))
            m_news.append(m_new)
            p_r = [jnp.exp2(t_r[:, c * HEAD_DIM:(c + 1) * HEAD_DIM] - m_new)
                   for c in range(width // HEAD_DIM)]
            ps.append(jnp.concatenate(p_r, axis=1).astype(BF16))
        o = jnp.dot(jnp.concatenate(ps, axis=0), v_ones, preferred_element_type=F32)
        for r in range(Q_PER_KV):
            acc = o[r * SPAN:(r + 1) * SPAN, :HEAD_DIM]
            den = o[r * SPAN:(r + 1) * SPAN, HEAD_DIM:]
            if src is not None:
                acc = src[0][r, src[3], :] * a_olds[r] + acc
                den = src[2][r, src[3], :] * a_olds[r] + den
            if dst is None:
                o_ref[q_idx, r * HEAD_DIM:(r + 1) * HEAD_DIM] = acc / den
            else:
                dst[0][r, dst[3], :] = acc
                dst[1][r, dst[3], :] = m_news[r]
                dst[2][r, dst[3], :] = den

    near, mid, far = DILATIONS
    quarter = seq // mid
    nb_mid = quarter // SPAN
    class_state = (tacc_ref, tm_ref, tl_ref)
    seq_state = (acc_ref, m_ref, l_ref)

    def far_block(e, carry):
        idx = pl.ds(e, SPAN, stride=far)
        dst_idx = pl.ds((e % mid) * quarter + e // mid, SPAN, stride=mid)
        block(idx, idx, cap_one, None, class_state + (dst_idx,))
        return carry

    lax.fori_loop(0, far, far_block, 0, unroll=ATTN_UNROLL)

    def mid_first(e4, carry):
        idx = pl.ds(e4, SPAN, stride=mid)
        src_idx = pl.ds(pl.multiple_of(e4 * quarter, SPAN), SPAN)
        block(idx, idx, cap_one, class_state + (src_idx,), seq_state + (idx,))
        return carry

    lax.fori_loop(0, mid, mid_first, 0, unroll=ATTN_UNROLL)

    def mid_later(t, carry):
        e4 = t // (nb_mid - 1)
        nblk = t % (nb_mid - 1) + 1
        start = e4 + nblk * (mid * SPAN)
        q_idx = pl.ds(start, SPAN, stride=mid)
        k_idx = pl.ds(start - mid * SPAN, 2 * SPAN, stride=mid)
        src_idx = pl.ds(pl.multiple_of(e4 * quarter + nblk * SPAN, SPAN), SPAN)
        block(q_idx, k_idx, cap_two, class_state + (src_idx,), seq_state + (q_idx,))
        return carry

    lax.fori_loop(0, mid * (nb_mid - 1), mid_later, 0, unroll=ATTN_UNROLL)

    first = pl.ds(0, SPAN)
    block(first, first, cap_one, seq_state + (first,), None)

    def near_later(nblk, carry):
        q_idx = pl.ds(pl.multiple_of(nblk * SPAN, SPAN), SPAN)
        k_idx = pl.ds(pl.multiple_of((nblk - 1) * SPAN, SPAN), 2 * SPAN)
        block(q_idx, k_idx, cap_two, seq_state + (q_idx,), None)
        return carry

    lax.fori_loop(1, seq // (near * SPAN), near_later, 0, unroll=ATTN_UNROLL)


def _attention(q, k, v, batch, seq):
    n = q.shape[1]
    assert seq % (DILATIONS[2] * SPAN) == 0
    slab = Q_PER_KV * seq * HEAD_DIM * 4
    vmem = 2 * slab + 2 * 2 * seq * HEAD_DIM * 4 + 2 * slab + 6 * slab
    return pl.pallas_call(
        _attn_kernel,
        grid=(batch, N_KV_HEADS),
        in_specs=[
            pl.BlockSpec((Q_PER_KV, seq, HEAD_DIM), lambda b, g: (g, b, 0)),
            pl.BlockSpec((None, seq, HEAD_DIM), lambda b, g: (g, b, 0)),
            pl.BlockSpec((None, seq, HEAD_DIM), lambda b, g: (g, b, 0)),
        ],
        out_specs=pl.BlockSpec((seq, Q_PER_KV * HEAD_DIM), lambda b, g: (b, g)),
        out_shape=jax.ShapeDtypeStruct((n, Q_DIM), F32),
        scratch_shapes=[pltpu.VMEM((Q_PER_KV, seq, HEAD_DIM), F32)] * 6,
        compiler_params=pltpu.CompilerParams(
            dimension_semantics=("parallel", "parallel"),
            vmem_limit_bytes=min(vmem + (8 << 20), V7X_VMEM_BYTES - (4 << 20))),
        name="attn",
    )(q, k, v)


def _out_proj_kernel(a_ref, cn_ref, x_ref, ga_ref, w_ref, gpost_ref, o_ref):
    an = _rms(a_ref[...], ga_ref[...]).astype(BF16)
    mixed = jnp.dot(an, w_ref[0:Q_DIM, :], preferred_element_type=F32)
    mixed += jnp.dot(cn_ref[...], w_ref[Q_DIM:, :], preferred_element_type=F32)
    o_ref[...] = x_ref[...] + _rms(mixed, gpost_ref[...])


def _out_proj(a, cn, x, ga, w_out, gpost, layer, *, tm=512):
    n, d = x.shape
    assert n % tm == 0
    vmem = (2 * tm * Q_DIM * 4 + 2 * tm * CONV_DIM * 2 + 4 * tm * d * 4
            + 2 * (Q_DIM + CONV_DIM) * d * 2 + 2 * tm * d * 4)
    return pl.pallas_call(
        _out_proj_kernel,
        grid=(n // tm,),
        in_specs=[
            pl.BlockSpec((tm, Q_DIM), lambda i: (i, 0)),
            pl.BlockSpec((tm, CONV_DIM), lambda i: (i, 0)),
            pl.BlockSpec((tm, d), lambda i: (i, 0)),
            pl.BlockSpec((None, 1, Q_DIM), lambda i: (layer, 0, 0)),
            pl.BlockSpec((None, Q_DIM + CONV_DIM, d), lambda i: (layer, 0, 0)),
            pl.BlockSpec((None, 1, d), lambda i: (layer, 0, 0)),
        ],
        out_specs=pl.BlockSpec((tm, d), lambda i: (i, 0)),
        out_shape=jax.ShapeDtypeStruct((n, d), F32),
        compiler_params=pltpu.CompilerParams(
            dimension_semantics=("parallel",),
            vmem_limit_bytes=min(vmem + (4 << 20), V7X_VMEM_BYTES - (4 << 20))),
        name="out_proj",
    )(a, cn, x, ga, w_out, gpost)


def kernel(x, ffn1_norm_pre, ffn1_w_gate_up, ffn1_w_down, ffn1_norm_post, mix_norm_pre, w_in, conv_w, attn_out_norm, conv_out_norm, w_out, mix_norm_post, ffn2_norm_pre, ffn2_w_gate_up, ffn2_w_down, ffn2_norm_post):
    batch, seq, d = x.shape
    depth = w_in.shape[0]
    xf = x.reshape(batch * seq, d)

    def row(gain):
        return gain.reshape(gain.shape[0], 1, gain.shape[1])

    ni, nj = (batch * seq) // FFN_TOKEN_TILE, ffn1_w_down.shape[1] // FFN_HIDDEN_TILE
    w1 = (ffn1_w_gate_up[0].astype(BF16), ffn1_w_down[0].astype(BF16))
    for l in range(depth):
        casts = [_cast_gate_up(ffn2_w_gate_up, l, ni, nj), _cast_down(ffn2_w_down, l, ni, nj)]
        if l == 0:
            casts += [_cast_all_layers(w_in, IN_PROJ_CAST_COLS, ni, nj),
                      _cast_all_layers(w_out, OUT_PROJ_CAST_COLS, ni, nj)]
        xf, done = _ffn(xf, row(ffn1_norm_pre), w1[0], w1[1], row(ffn1_norm_post), l, casts)
        w2 = done[:2]
        if l == 0:
            w_in_b = done[2].reshape(w_in.shape)
            w_out_b = done[3].reshape(w_out.shape)
        q, k, v, cn = _in_proj(xf, row(mix_norm_pre), w_in_b, conv_w, row(conv_out_norm), l, seq)
        a = _attention(q, k, v, batch, seq)
        xf = _out_proj(a, cn, xf, row(attn_out_norm), w_out_b, row(mix_norm_post), l)
        casts = []
        if l + 1 < depth:
            casts = [_cast_gate_up(ffn1_w_gate_up, l + 1, ni, nj), _cast_down(ffn1_w_down, l + 1, ni, nj)]
        xf, w1 = _ffn(xf, row(ffn2_norm_pre), w2[0], w2[1], row(ffn2_norm_post), l, casts)
    return xf.reshape(batch, seq, d)
```

```python
import functools
from typing import Callable, NamedTuple

import jax
import jax.numpy as jnp
from jax import lax
from jax.experimental import pallas as pl
from jax.experimental.pallas import tpu as pltpu

D_MODEL = 2048
HEAD_DIM = 128
ATTN_WIDTH = D_MODEL // 2
CONV_DIM = D_MODEL - ATTN_WIDTH
N_Q_HEADS = ATTN_WIDTH // HEAD_DIM
N_KV_HEADS = max(1, N_Q_HEADS // 4)
Q_PER_KV = N_Q_HEADS // N_KV_HEADS
Q_DIM = N_Q_HEADS * HEAD_DIM
KV_DIM = N_KV_HEADS * HEAD_DIM
CONV_WIDTH = 3
IN_PROJ_DIM = Q_DIM + 2 * KV_DIM + 3 * CONV_DIM
SPAN = 128
DILATIONS = (1, 4, 16)
FFN_RESIDUAL_WEIGHT = 0.5
NORM_EPS = 1e-6
LOG2_E = 1.4426950408889634

V7X_VMEM_BYTES = 64 * 1024 * 1024
SUBLANES = 8
FFN_TOKEN_TILE = 1024
FFN_HIDDEN_TILE = 512
FFN_ROW_SUBTILE = 512
PROJ_ROW_SUBTILE = 256
CAST_ROWS = 32
GATE_UP_CAST_ROWS = 16
NORM_ROW_CHUNK = 128
ATTN_UNROLL = 8

BF16 = jnp.bfloat16
F32 = jnp.float32

_OFF_K = Q_DIM
_OFF_V = Q_DIM + KV_DIM
_OFF_H = Q_DIM + 2 * KV_DIM
_OFF_B = _OFF_H + CONV_DIM
_OFF_C = _OFF_B + CONV_DIM


def _rms(x, gain):
    y = x * lax.rsqrt(jnp.mean(x * x, axis=-1, keepdims=True) + NORM_EPS)
    return y * gain


def _ffn_kernel(*refs, n_casts):
    x_ref, gpre_ref, wg_ref, wu_ref, wd_ref, gpost_ref = refs[:6]
    cast_src = refs[6:6 + n_casts]
    o_ref = refs[6 + n_casts]
    cast_dst = refs[7 + n_casts:7 + 2 * n_casts]
    h_ref = refs[7 + 2 * n_casts]
    j = pl.program_id(1)
    last = pl.num_programs(1) - 1
    n_rows = h_ref.shape[0]

    def step(first_step, last_step):
        for src, dst in zip(cast_src, cast_dst):
            dst[...] = src[...].astype(BF16)
        if last_step:
            half_gain = FFN_RESIDUAL_WEIGHT * gpost_ref[...]
        for r0 in range(0, n_rows, FFN_ROW_SUBTILE):
            if first_step:
                for c0 in range(r0, r0 + FFN_ROW_SUBTILE, NORM_ROW_CHUNK):
                    rows = slice(c0, c0 + NORM_ROW_CHUNK)
                    h_ref[rows, :] = _rms(x_ref[rows, :], gpre_ref[...]).astype(BF16)
            rows = slice(r0, r0 + FFN_ROW_SUBTILE)
            h = h_ref[rows, :]
            g = jnp.dot(h, wg_ref[...], preferred_element_type=F32)
            u = jnp.dot(h, wu_ref[...], preferred_element_type=F32)
            a = (g * jax.nn.sigmoid(g) * u).astype(BF16)
            part = jnp.dot(a, wd_ref[...], preferred_element_type=F32)
            if first_step:
                o_ref[rows, :] = part
            else:
                o_ref[rows, :] += part
            if last_step:
                for c0 in range(r0, r0 + FFN_ROW_SUBTILE, NORM_ROW_CHUNK):
                    rows = slice(c0, c0 + NORM_ROW_CHUNK)
                    o_ref[rows, :] = x_ref[rows, :] + _rms(o_ref[rows, :], half_gain)

    pl.when(j == 0)(lambda: step(True, False))
    pl.when((j > 0) & (j < last))(lambda: step(False, False))
    pl.when(j == last)(lambda: step(False, True))


class _Cast(NamedTuple):
    src: jax.Array
    src_block: tuple
    src_map: Callable
    out_shape: tuple
    out_block: tuple
    out_map: Callable


def _cast_rows(w, layer, block_rows, nj):
    depth, rows, cols = w.shape
    if layer is None:
        w, rows = w.reshape(depth * rows, cols), depth * rows
    assert rows % block_rows == 0
    last_block = rows // block_rows - 1

    def out_map(i, j):
        return (jnp.minimum(i * nj + j, last_block), 0)

    if layer is None:
        return _Cast(w, (block_rows, cols), out_map, (rows, cols), (block_rows, cols), out_map)
    return _Cast(w, (None, block_rows, cols), lambda i, j: (layer,) + out_map(i, j),
                 (rows, cols), (block_rows, cols), out_map)


def _ffn(x, gpre, w_gu, w_down, gpost, layer, casts=()):
    n, d = x.shape
    d_ff = w_down.shape[0]
    tm, tf = FFN_TOKEN_TILE, FFN_HIDDEN_TILE
    nj = d_ff // tf
    assert n % tm == 0 and d_ff % tf == 0
    assert all(c.out_shape[0] // c.out_block[0] <= (n // tm) * nj for c in casts)
    cast_bytes = sum(2 * (4 + 2) * c.out_block[0] * c.out_block[1] for c in casts)
    vmem = (2 * 2 * tm * d * 4
            + tm * d * 2
            + 2 * 3 * d * tf * 2
            + 6 * FFN_ROW_SUBTILE * tf * 4
            + cast_bytes)
    outs = pl.pallas_call(
        functools.partial(_ffn_kernel, n_casts=len(casts)),
        grid=(n // tm, nj),
        in_specs=[
            pl.BlockSpec((tm, d), lambda i, j: (i, 0)),
            pl.BlockSpec((None, 1, d), lambda i, j: (layer, 0, 0)),
            pl.BlockSpec((d, tf), lambda i, j: (0, j)),
            pl.BlockSpec((d, tf), lambda i, j: (0, j + nj)),
            pl.BlockSpec((tf, d), lambda i, j: (j, 0)),
            pl.BlockSpec((None, 1, d), lambda i, j: (layer, 0, 0)),
        ] + [pl.BlockSpec(c.src_block, c.src_map) for c in casts],
        out_specs=[pl.BlockSpec((tm, d), lambda i, j: (i, 0))]
        + [pl.BlockSpec(c.out_block, c.out_map) for c in casts],
        out_shape=[jax.ShapeDtypeStruct((n, d), F32)]
        + [jax.ShapeDtypeStruct(c.out_shape, BF16) for c in casts],
        scratch_shapes=[pltpu.VMEM((tm, d), BF16)],
        compiler_params=pltpu.CompilerParams(
            dimension_semantics=("arbitrary", "arbitrary"),
            vmem_limit_bytes=min(vmem + (4 << 20), V7X_VMEM_BYTES - (4 << 20))),
        name="ffn",
    )(x, gpre, w_gu, w_gu, w_down, gpost, *[c.src for c in casts])
    return outs[0], outs[1:]


def _in_proj_kernel(x_ref, gpre_ref, w_ref, cw_ref, gc_ref, q_ref, k_ref, v_ref, cn_ref,
                    u_ref, *, tiles_per_seq):
    i = pl.program_id(0)
    tm = x_ref.shape[0]

    @pl.when(i % tiles_per_seq == 0)
    def _():
        u_ref[0:SUBLANES, :] = jnp.zeros((SUBLANES, CONV_DIM), F32)

    @pl.when(i % tiles_per_seq != 0)
    def _():
        u_ref[0:SUBLANES, :] = u_ref[tm:tm + SUBLANES, :]

    for r0 in range(0, tm, PROJ_ROW_SUBTILE):
        rows = slice(r0, r0 + PROJ_ROW_SUBTILE)
        h = _rms(x_ref[rows, :], gpre_ref[...]).astype(BF16)

        def proj(off, width):
            return jnp.dot(h, w_ref[:, off:off + width], preferred_element_type=F32)

        zq = proj(0, Q_DIM)
        for hd in range(N_Q_HEADS):
            q_ref[hd, rows, :] = zq[:, hd * HEAD_DIM:(hd + 1) * HEAD_DIM]
        zk = proj(_OFF_K, KV_DIM)
        zv = proj(_OFF_V, KV_DIM)
        for hd in range(N_KV_HEADS):
            k_ref[hd, rows, :] = zk[:, hd * HEAD_DIM:(hd + 1) * HEAD_DIM]
            v_ref[hd, rows, :] = zv[:, hd * HEAD_DIM:(hd + 1) * HEAD_DIM]

        base = SUBLANES + r0
        u_ref[base:base + PROJ_ROW_SUBTILE, :] = proj(_OFF_C, CONV_DIM) * proj(_OFF_H, CONV_DIM)
        y = (cw_ref[2:3, :] * u_ref[base:base + PROJ_ROW_SUBTILE, :]
             + cw_ref[1:2, :] * u_ref[base - 1:base - 1 + PROJ_ROW_SUBTILE, :]
             + cw_ref[0:1, :] * u_ref[base - 2:base - 2 + PROJ_ROW_SUBTILE, :])
        c = proj(_OFF_B, CONV_DIM) * y
        cn_ref[rows, :] = _rms(c, gc_ref[...]).astype(BF16)


def _in_proj(x, gpre, w_in, conv_w, gconv, layer, seq, *, tm=512):
    n, d = x.shape
    assert n % tm == 0 and seq % tm == 0
    vmem = (2 * tm * d * 4 + 2 * d * IN_PROJ_DIM * 2
            + 2 * tm * (Q_DIM + 2 * KV_DIM) * 4 + 2 * tm * CONV_DIM * 2
            + (tm + SUBLANES) * CONV_DIM * 4 + 4 * tm * CONV_DIM * 4)
    return pl.pallas_call(
        functools.partial(_in_proj_kernel, tiles_per_seq=seq // tm),
        grid=(n // tm,),
        in_specs=[
            pl.BlockSpec((tm, d), lambda i: (i, 0)),
            pl.BlockSpec((None, 1, d), lambda i: (layer, 0, 0)),
            pl.BlockSpec((None, d, IN_PROJ_DIM), lambda i: (layer, 0, 0)),
            pl.BlockSpec((None, CONV_WIDTH, CONV_DIM), lambda i: (layer, 0, 0)),
            pl.BlockSpec((None, 1, CONV_DIM), lambda i: (layer, 0, 0)),
        ],
        out_specs=[
            pl.BlockSpec((N_Q_HEADS, tm, HEAD_DIM), lambda i: (0, i, 0)),
            pl.BlockSpec((N_KV_HEADS, tm, HEAD_DIM), lambda i: (0, i, 0)),
            pl.BlockSpec((N_KV_HEADS, tm, HEAD_DIM), lambda i: (0, i, 0)),
            pl.BlockSpec((tm, CONV_DIM), lambda i: (i, 0)),
        ],
        out_shape=[
            jax.ShapeDtypeStruct((N_Q_HEADS, n, HEAD_DIM), F32),
            jax.ShapeDtypeStruct((N_KV_HEADS, n, HEAD_DIM), F32),
            jax.ShapeDtypeStruct((N_KV_HEADS, n, HEAD_DIM), F32),
            jax.ShapeDtypeStruct((n, CONV_DIM), BF16),
        ],
        scratch_shapes=[pltpu.VMEM((tm + SUBLANES, CONV_DIM), F32)],
        compiler_params=pltpu.CompilerParams(
            dimension_semantics=("arbitrary",),
            vmem_limit_bytes=min(vmem + (4 << 20), V7X_VMEM_BYTES - (4 << 20))),
        name="in_proj",
    )(x, gpre, w_in, conv_w, gconv)


def _attn_kernel(q_ref, k_ref, v_ref, o_ref, acc_ref, m_ref, l_ref, tacc_ref, tm_ref, tl_ref):
    seq = k_ref.shape[0]
    logit_scale = (HEAD_DIM ** -0.5) * LOG2_E
    rows = Q_PER_KV * SPAN
    qi = lax.broadcasted_iota(jnp.int32, (rows, 2 * SPAN), 0) % SPAN
    kj = lax.broadcasted_iota(jnp.int32, (rows, 2 * SPAN), 1)
    cap_two = jnp.where((kj >= qi) & (kj <= qi + SPAN), jnp.inf, -jnp.inf)
    qi1 = lax.broadcasted_iota(jnp.int32, (rows, SPAN), 0) % SPAN
    kj1 = lax.broadcasted_iota(jnp.int32, (rows, SPAN), 1)
    cap_one = jnp.where(kj1 <= qi1, jnp.inf, -jnp.inf)

    def block(q_idx, k_idx, cap, src, dst):
        width = cap.shape[1]
        q = jnp.concatenate([q_ref[r, q_idx, :] for r in range(Q_PER_KV)], axis=0).astype(BF16)
        k = k_ref[k_idx, :].astype(BF16)
        v = v_ref[k_idx, :].astype(BF16)
        v_ones = jnp.concatenate([v, jnp.ones((width, HEAD_DIM), BF16)], axis=1)
        t = lax.dot_general(q, k, (((1,), (1,)), ((), ())), preferred_element_type=F32)
        t = jnp.minimum(t * logit_scale, cap)
        ps, m_news, a_olds = [], [], []
        for r in range(Q_PER_KV):
            t_r = t[r * SPAN:(r + 1) * SPAN]
            m_blk = jnp.broadcast_to(jnp.max(t_r, axis=-1, keepdims=True), (SPAN, HEAD_DIM))
            if src is None:
                m_new = m_blk
            else:
                m_old = src[1][r, src[3], :]
                m_new = jnp.maximum(m_old, m_blk)
                a_olds.append(jnp.exp2(m_old - m_new))
            m_news.append(m_new)
            p_r = [jnp.exp2(t_r[:, c * HEAD_DIM:(c + 1) * HEAD_DIM] - m_new)
                   for c in range(width // HEAD_DIM)]
            ps.append(jnp.concatenate(p_r, axis=1).astype(BF16))
        o = jnp.dot(jnp.concatenate(ps, axis=0), v_ones, preferred_element_type=F32)
        for r in range(Q_PER_KV):
            acc = o[r * SPAN:(r + 1) * SPAN, :HEAD_DIM]
            den = o[r * SPAN:(r + 1) * SPAN, HEAD_DIM:]
            if src is not None:
                acc = src[0][r, src[3], :] * a_olds[r] + acc
                den = src[2][r, src[3], :] * a_olds[r] + den
            if dst is None:
                o_ref[q_idx, r * HEAD_DIM:(r + 1) * HEAD_DIM] = acc / den
            else:
                dst[0][r, dst[3], :] = acc
                dst[1][r, dst[3], :] = m_news[r]
                dst[2][r, dst[3], :] = den

    near, mid, far = DILATIONS
    quarter = seq // mid
    nb_mid = quarter // SPAN
    class_state = (tacc_ref, tm_ref, tl_ref)
    seq_state = (acc_ref, m_ref, l_ref)

    def far_block(e, carry):
        idx = pl.ds(e, SPAN, stride=far)
        dst_idx = pl.ds((e % mid) * quarter + e // mid, SPAN, stride=mid)
        block(idx, idx, cap_one, None, class_state + (dst_idx,))
        return carry

    lax.fori_loop(0, far, far_block, 0, unroll=ATTN_UNROLL)

    def mid_first(e4, carry):
        idx = pl.ds(e4, SPAN, stride=mid)
        src_idx = pl.ds(pl.multiple_of(e4 * quarter, SPAN), SPAN)
        block(idx, idx, cap_one, class_state + (src_idx,), seq_state + (idx,))
        return carry

    lax.fori_loop(0, mid, mid_first, 0, unroll=ATTN_UNROLL)

    def mid_later(t, carry):
        e4 = t // (nb_mid - 1)
        nblk = t % (nb_mid - 1) + 1
        start = e4 + nblk * (mid * SPAN)
        q_idx = pl.ds(start, SPAN, stride=mid)
        k_idx = pl.ds(start - mid * SPAN, 2 * SPAN, stride=mid)
        src_idx = pl.ds(pl.multiple_of(e4 * quarter + nblk * SPAN, SPAN), SPAN)
        block(q_idx, k_idx, cap_two, class_state + (src_idx,), seq_state + (q_idx,))
        return carry

    lax.fori_loop(0, mid * (nb_mid - 1), mid_later, 0, unroll=ATTN_UNROLL)

    first = pl.ds(0, SPAN)
    block(first, first, cap_one, seq_state + (first,), None)

    def near_later(nblk, carry):
        q_idx = pl.ds(pl.multiple_of(nblk * SPAN, SPAN), SPAN)
        k_idx = pl.ds(pl.multiple_of((nblk - 1) * SPAN, SPAN), 2 * SPAN)
        block(q_idx, k_idx, cap_two, seq_state + (q_idx,), None)
        return carry

    lax.fori_loop(1, seq // (near * SPAN), near_later, 0, unroll=ATTN_UNROLL)


def _attention(q, k, v, batch, seq):
    n = q.shape[1]
    assert seq % (DILATIONS[2] * SPAN) == 0
    slab = Q_PER_KV * seq * HEAD_DIM * 4
    vmem = 2 * slab + 2 * 2 * seq * HEAD_DIM * 4 + 2 * slab + 6 * slab
    return pl.pallas_call(
        _attn_kernel,
        grid=(batch, N_KV_HEADS),
        in_specs=[
            pl.BlockSpec((Q_PER_KV, seq, HEAD_DIM), lambda b, g: (g, b, 0)),
            pl.BlockSpec((None, seq, HEAD_DIM), lambda b, g: (g, b, 0)),
            pl.BlockSpec((None, seq, HEAD_DIM), lambda b, g: (g, b, 0)),
        ],
        out_specs=pl.BlockSpec((seq, Q_PER_KV * HEAD_DIM), lambda b, g: (b, g)),
        out_shape=jax.ShapeDtypeStruct((n, Q_DIM), F32),
        scratch_shapes=[pltpu.VMEM((Q_PER_KV, seq, HEAD_DIM), F32)] * 6,
        compiler_params=pltpu.CompilerParams(
            dimension_semantics=("parallel", "parallel"),
            vmem_limit_bytes=min(vmem + (8 << 20), V7X_VMEM_BYTES - (4 << 20))),
        name="attn",
    )(q, k, v)


def _out_proj_kernel(a_ref, cn_ref, x_ref, ga_ref, w_ref, gpost_ref, o_ref):
    an = _rms(a_ref[...], ga_ref[...]).astype(BF16)
    mixed = jnp.dot(an, w_ref[0:Q_DIM, :], preferred_element_type=F32)
    mixed += jnp.dot(cn_ref[...], w_ref[Q_DIM:, :], preferred_element_type=F32)
    o_ref[...] = x_ref[...] + _rms(mixed, gpost_ref[...])


def _out_proj(a, cn, x, ga, w_out, gpost, layer, *, tm=512):
    n, d = x.shape
    assert n % tm == 0
    vmem = (2 * tm * Q_DIM * 4 + 2 * tm * CONV_DIM * 2 + 4 * tm * d * 4
            + 2 * (Q_DIM + CONV_DIM) * d * 2 + 2 * tm * d * 4)
    return pl.pallas_call(
        _out_proj_kernel,
        grid=(n // tm,),
        in_specs=[
            pl.BlockSpec((tm, Q_DIM), lambda i: (i, 0)),
            pl.BlockSpec((tm, CONV_DIM), lambda i: (i, 0)),
            pl.BlockSpec((tm, d), lambda i: (i, 0)),
            pl.BlockSpec((None, 1, Q_DIM), lambda i: (layer, 0, 0)),
            pl.BlockSpec((None, Q_DIM + CONV_DIM, d), lambda i: (layer, 0, 0)),
            pl.BlockSpec((None, 1, d), lambda i: (layer, 0, 0)),
        ],
        out_specs=pl.BlockSpec((tm, d), lambda i: (i, 0)),
        out_shape=jax.ShapeDtypeStruct((n, d), F32),
        compiler_params=pltpu.CompilerParams(
            dimension_semantics=("parallel",),
            vmem_limit_bytes=min(vmem + (4 << 20), V7X_VMEM_BYTES - (4 << 20))),
        name="out_proj",
    )(a, cn, x, ga, w_out, gpost)


def kernel(x, ffn1_norm_pre, ffn1_w_gate_up, ffn1_w_down, ffn1_norm_post, mix_norm_pre, w_in, conv_w, attn_out_norm, conv_out_norm, w_out, mix_norm_post, ffn2_norm_pre, ffn2_w_gate_up, ffn2_w_down, ffn2_norm_post):
    batch, seq, d = x.shape
    depth = w_in.shape[0]
    xf = x.reshape(batch * seq, d)

    def row(gain):
        return gain.reshape(gain.shape[0], 1, gain.shape[1])

    nj = ffn1_w_down.shape[1] // FFN_HIDDEN_TILE
    w1 = (ffn1_w_gate_up[0].astype(BF16), ffn1_w_down[0].astype(BF16))
    for l in range(depth):
        casts = [_cast_rows(ffn2_w_gate_up, l, GATE_UP_CAST_ROWS, nj),
                 _cast_rows(ffn2_w_down, l, CAST_ROWS, nj)]
        if l == 0:
            casts += [_cast_rows(w_in, None, CAST_ROWS, nj), _cast_rows(w_out, None, CAST_ROWS, nj)]
        xf, done = _ffn(xf, row(ffn1_norm_pre), w1[0], w1[1], row(ffn1_norm_post), l, casts)
        w2 = done[:2]
        if l == 0:
            w_in_b = done[2].reshape(w_in.shape)
            w_out_b = done[3].reshape(w_out.shape)
        q, k, v, cn = _in_proj(xf, row(mix_norm_pre), w_in_b, conv_w, row(conv_out_norm), l, seq)
        a = _attention(q, k, v, batch, seq)
        xf = _out_proj(a, cn, xf, row(attn_out_norm), w_out_b, row(mix_norm_post), l)
        casts = []
        if l + 1 < depth:
            casts = [_cast_rows(ffn1_w_gate_up, l + 1, GATE_UP_CAST_ROWS, nj),
                     _cast_rows(ffn1_w_down, l + 1, CAST_ROWS, nj)]
        xf, w1 = _ffn(xf, row(ffn2_norm_pre), w2[0], w2[1], row(ffn2_norm_post), l, casts)
    return xf.reshape(batch, seq, d)
```

```python
import functools
from typing import Callable, NamedTuple

import jax
import jax.numpy as jnp
from jax import lax
from jax.experimental import pallas as pl
from jax.experimental.pallas import tpu as pltpu

D_MODEL = 2048
HEAD_DIM = 128
ATTN_WIDTH = D_MODEL // 2
CONV_DIM = D_MODEL - ATTN_WIDTH
N_Q_HEADS = ATTN_WIDTH // HEAD_DIM
N_KV_HEADS = max(1, N_Q_HEADS // 4)
Q_PER_KV = N_Q_HEADS // N_KV_HEADS
Q_DIM = N_Q_HEADS * HEAD_DIM
KV_DIM = N_KV_HEADS * HEAD_DIM
CONV_WIDTH = 3
IN_PROJ_DIM = Q_DIM + 2 * KV_DIM + 3 * CONV_DIM
SPAN = 128
DILATIONS = (1, 4, 16)
FFN_RESIDUAL_WEIGHT = 0.5
NORM_EPS = 1e-6
LOG2_E = 1.4426950408889634

V7X_VMEM_BYTES = 64 * 1024 * 1024
SUBLANES = 8
FFN_TOKEN_TILE = 1024
FFN_HIDDEN_TILE = 512
FFN_ROW_SUBTILE = 512
PROJ_ROW_SUBTILE = 256
CAST_ROWS = 32
GATE_UP_CAST_ROWS = 16
NORM_ROW_CHUNK = 128
ATTN_UNROLL = 16

BF16 = jnp.bfloat16
F32 = jnp.float32

_OFF_K = Q_DIM
_OFF_V = Q_DIM + KV_DIM
_OFF_H = Q_DIM + 2 * KV_DIM
_OFF_B = _OFF_H + CONV_DIM
_OFF_C = _OFF_B + CONV_DIM


def _rms(x, gain):
    y = x * lax.rsqrt(jnp.mean(x * x, axis=-1, keepdims=True) + NORM_EPS)
    return y * gain


def _ffn_kernel(*refs, n_casts):
    x_ref, gpre_ref, wg_ref, wu_ref, wd_ref, gpost_ref = refs[:6]
    cast_src = refs[6:6 + n_casts]
    o_ref = refs[6 + n_casts]
    cast_dst = refs[7 + n_casts:7 + 2 * n_casts]
    h_ref = refs[7 + 2 * n_casts]
    j = pl.program_id(1)
    last = pl.num_programs(1) - 1
    n_rows = h_ref.shape[0]

    def step(first_step, last_step):
        for src, dst in zip(cast_src, cast_dst):
            dst[...] = src[...].astype(BF16)
        if last_step:
            half_gain = FFN_RESIDUAL_WEIGHT * gpost_ref[...]
        for r0 in range(0, n_rows, FFN_ROW_SUBTILE):
            if first_step:
                for c0 in range(r0, r0 + FFN_ROW_SUBTILE, NORM_ROW_CHUNK):
                    rows = slice(c0, c0 + NORM_ROW_CHUNK)
                    h_ref[rows, :] = _rms(x_ref[rows, :], gpre_ref[...]).astype(BF16)
            rows = slice(r0, r0 + FFN_ROW_SUBTILE)
            h = h_ref[rows, :]
            g = jnp.dot(h, wg_ref[...], preferred_element_type=F32)
            u = jnp.dot(h, wu_ref[...], preferred_element_type=F32)
            a = (g * jax.nn.sigmoid(g) * u).astype(BF16)
            part = jnp.dot(a, wd_ref[...], preferred_element_type=F32)
            if first_step:
                o_ref[rows, :] = part
            else:
                o_ref[rows, :] += part
            if last_step:
                for c0 in range(r0, r0 + FFN_ROW_SUBTILE, NORM_ROW_CHUNK):
                    rows = slice(c0, c0 + NORM_ROW_CHUNK)
                    o_ref[rows, :] = x_ref[rows, :] + _rms(o_ref[rows, :], half_gain)

    pl.when(j == 0)(lambda: step(True, False))
    pl.when((j > 0) & (j < last))(lambda: step(False, False))
    pl.when(j == last)(lambda: step(False, True))


class _Cast(NamedTuple):
    src: jax.Array
    src_block: tuple
    src_map: Callable
    out_shape: tuple
    out_block: tuple
    out_map: Callable


def _cast_rows(w, layer, block_rows, nj):
    depth, rows, cols = w.shape
    if layer is None:
        w, rows = w.reshape(depth * rows, cols), depth * rows
    assert rows % block_rows == 0
    last_block = rows // block_rows - 1

    def out_map(i, j):
        return (jnp.minimum(i * nj + j, last_block), 0)

    if layer is None:
        return _Cast(w, (block_rows, cols), out_map, (rows, cols), (block_rows, cols), out_map)
    return _Cast(w, (None, block_rows, cols), lambda i, j: (layer,) + out_map(i, j),
                 (rows, cols), (block_rows, cols), out_map)


def _ffn(x, gpre, w_gu, w_down, gpost, layer, casts=()):
    n, d = x.shape
    d_ff = w_down.shape[0]
    tm, tf = FFN_TOKEN_TILE, FFN_HIDDEN_TILE
    nj = d_ff // tf
    assert n % tm == 0 and d_ff % tf == 0
    assert all(c.out_shape[0] // c.out_block[0] <= (n // tm) * nj for c in casts)
    cast_bytes = sum(2 * (4 + 2) * c.out_block[0] * c.out_block[1] for c in casts)
    vmem = (2 * 2 * tm * d * 4
            + tm * d * 2
            + 2 * 3 * d * tf * 2
            + 6 * FFN_ROW_SUBTILE * tf * 4
            + cast_bytes)
    outs = pl.pallas_call(
        functools.partial(_ffn_kernel, n_casts=len(casts)),
        grid=(n // tm, nj),
        in_specs=[
            pl.BlockSpec((tm, d), lambda i, j: (i, 0)),
            pl.BlockSpec((None, 1, d), lambda i, j: (layer, 0, 0)),
            pl.BlockSpec((d, tf), lambda i, j: (0, j)),
            pl.BlockSpec((d, tf), lambda i, j: (0, j + nj)),
            pl.BlockSpec((tf, d), lambda i, j: (j, 0)),
            pl.BlockSpec((None, 1, d), lambda i, j: (layer, 0, 0)),
        ] + [pl.BlockSpec(c.src_block, c.src_map) for c in casts],
        out_specs=[pl.BlockSpec((tm, d), lambda i, j: (i, 0))]
        + [pl.BlockSpec(c.out_block, c.out_map) for c in casts],
        out_shape=[jax.ShapeDtypeStruct((n, d), F32)]
        + [jax.ShapeDtypeStruct(c.out_shape, BF16) for c in casts],
        scratch_shapes=[pltpu.VMEM((tm, d), BF16)],
        compiler_params=pltpu.CompilerParams(
            dimension_semantics=("arbitrary", "arbitrary"),
            vmem_limit_bytes=min(vmem + (4 << 20), V7X_VMEM_BYTES - (4 << 20))),
        name="ffn",
    )(x, gpre, w_gu, w_gu, w_down, gpost, *[c.src for c in casts])
    return outs[0], outs[1:]


def _in_proj_kernel(x_ref, gpre_ref, w_ref, cw_ref, gc_ref, q_ref, k_ref, v_ref, cn_ref,
                    u_ref, *, tiles_per_seq):
    i = pl.program_id(0)
    tm = x_ref.shape[0]

    @pl.when(i % tiles_per_seq == 0)
    def _():
        u_ref[0:SUBLANES, :] = jnp.zeros((SUBLANES, CONV_DIM), F32)

    @pl.when(i % tiles_per_seq != 0)
    def _():
        u_ref[0:SUBLANES, :] = u_ref[tm:tm + SUBLANES, :]

    for r0 in range(0, tm, PROJ_ROW_SUBTILE):
        rows = slice(r0, r0 + PROJ_ROW_SUBTILE)
        h = _rms(x_ref[rows, :], gpre_ref[...]).astype(BF16)

        def proj(off, width):
            return jnp.dot(h, w_ref[:, off:off + width], preferred_element_type=F32)

        base = SUBLANES + r0
        u_ref[base:base + PROJ_ROW_SUBTILE, :] = proj(_OFF_C, CONV_DIM) * proj(_OFF_H, CONV_DIM)
        y = (cw_ref[2:3, :] * u_ref[base:base + PROJ_ROW_SUBTILE, :]
             + cw_ref[1:2, :] * u_ref[base - 1:base - 1 + PROJ_ROW_SUBTILE, :]
             + cw_ref[0:1, :] * u_ref[base - 2:base - 2 + PROJ_ROW_SUBTILE, :])
        c = proj(_OFF_B, CONV_DIM) * y
        cn_ref[rows, :] = _rms(c, gc_ref[...]).astype(BF16)

        zq = proj(0, Q_DIM)
        for hd in range(N_Q_HEADS):
            q_ref[hd, rows, :] = zq[:, hd * HEAD_DIM:(hd + 1) * HEAD_DIM]
        zk = proj(_OFF_K, KV_DIM)
        zv = proj(_OFF_V, KV_DIM)
        for hd in range(N_KV_HEADS):
            k_ref[hd, rows, :] = zk[:, hd * HEAD_DIM:(hd + 1) * HEAD_DIM]
            v_ref[hd, rows, :] = zv[:, hd * HEAD_DIM:(hd + 1) * HEAD_DIM]


def _in_proj(x, gpre, w_in, conv_w, gconv, layer, seq, *, tm=512):
    n, d = x.shape
    assert n % tm == 0 and seq % tm == 0
    vmem = (2 * tm * d * 4 + 2 * d * IN_PROJ_DIM * 2
            + 2 * tm * (Q_DIM + 2 * KV_DIM) * 4 + 2 * tm * CONV_DIM * 2
            + (tm + SUBLANES) * CONV_DIM * 4 + 4 * tm * CONV_DIM * 4)
    return pl.pallas_call(
        functools.partial(_in_proj_kernel, tiles_per_seq=seq // tm),
        grid=(n // tm,),
        in_specs=[
            pl.BlockSpec((tm, d), lambda i: (i, 0)),
            pl.BlockSpec((None, 1, d), lambda i: (layer, 0, 0)),
            pl.BlockSpec((None, d, IN_PROJ_DIM), lambda i: (layer, 0, 0)),
            pl.BlockSpec((None, CONV_WIDTH, CONV_DIM), lambda i: (layer, 0, 0)),
            pl.BlockSpec((None, 1, CONV_DIM), lambda i: (layer, 0, 0)),
        ],
        out_specs=[
            pl.BlockSpec((N_Q_HEADS, tm, HEAD_DIM), lambda i: (0, i, 0)),
            pl.BlockSpec((N_KV_HEADS, tm, HEAD_DIM), lambda i: (0, i, 0)),
            pl.BlockSpec((N_KV_HEADS, tm, HEAD_DIM), lambda i: (0, i, 0)),
            pl.BlockSpec((tm, CONV_DIM), lambda i: (i, 0)),
        ],
        out_shape=[
            jax.ShapeDtypeStruct((N_Q_HEADS, n, HEAD_DIM), F32),
            jax.ShapeDtypeStruct((N_KV_HEADS, n, HEAD_DIM), F32),
            jax.ShapeDtypeStruct((N_KV_HEADS, n, HEAD_DIM), F32),
            jax.ShapeDtypeStruct((n, CONV_DIM), BF16),
        ],
        scratch_shapes=[pltpu.VMEM((tm + SUBLANES, CONV_DIM), F32)],
        compiler_params=pltpu.CompilerParams(
            dimension_semantics=("arbitrary",),
            vmem_limit_bytes=min(vmem + (4 << 20), V7X_VMEM_BYTES - (4 << 20))),
        name="in_proj",
    )(x, gpre, w_in, conv_w, gconv)


def _attn_kernel(q_ref, k_ref, v_ref, o_ref, acc_ref, m_ref, l_ref, tacc_ref, tm_ref, tl_ref):
    seq = k_ref.shape[0]
    logit_scale = (HEAD_DIM ** -0.5) * LOG2_E
    rows = Q_PER_KV * SPAN
    qi = lax.broadcasted_iota(jnp.int32, (rows, 2 * SPAN), 0) % SPAN
    kj = lax.broadcasted_iota(jnp.int32, (rows, 2 * SPAN), 1)
    cap_two = jnp.where((kj >= qi) & (kj <= qi + SPAN), jnp.inf, -jnp.inf)
    qi1 = lax.broadcasted_iota(jnp.int32, (rows, SPAN), 0) % SPAN
    kj1 = lax.broadcasted_iota(jnp.int32, (rows, SPAN), 1)
    cap_one = jnp.where(kj1 <= qi1, jnp.inf, -jnp.inf)

    def block(q_idx, k_idx, cap, src, dst):
        width = cap.shape[1]
        q = jnp.concatenate([q_ref[r, q_idx, :] for r in range(Q_PER_KV)], axis=0).astype(BF16)
        k = k_ref[k_idx, :].astype(BF16)
        v = v_ref[k_idx, :].astype(BF16)
        v_ones = jnp.concatenate([v, jnp.ones((width, HEAD_DIM), BF16)], axis=1)
        t = lax.dot_general(q, k, (((1,), (1,)), ((), ())), preferred_element_type=F32)
        t = jnp.minimum(t * logit_scale, cap)
        ps, m_news, a_olds = [], [], []
        for r in range(Q_PER_KV):
            t_r = t[r * SPAN:(r + 1) * SPAN]
            m_blk = jnp.broadcast_to(jnp.max(t_r, axis=-1, keepdims=True), (SPAN, HEAD_DIM))
            if src is None:
                m_new = m_blk
            else:
                m_old = src[1][r, src[3], :]
                m_new = jnp.maximum(m_old, m_blk)
                a_olds.append(jnp.exp2(m_old - m_new))
            m_news.append(m_new)
            p_r = [jnp.exp2(t_r[:, c * HEAD_DIM:(c + 1) * HEAD_DIM] - m_new)
                   for c in range(width // HEAD_DIM)]
            ps.append(jnp.concatenate(p_r, axis=1).astype(BF16))
        o = jnp.dot(jnp.concatenate(ps, axis=0), v_ones, preferred_element_type=F32)
        for r in range(Q_PER_KV):
            acc = o[r * SPAN:(r + 1) * SPAN, :HEAD_DIM]
            den = o[r * SPAN:(r + 1) * SPAN, HEAD_DIM:]
            if src is not None:
                acc = src[0][r, src[3], :] * a_olds[r] + acc
                den = src[2][r, src[3], :] * a_olds[r] + den
            if dst is None:
                o_ref[q_idx, r * HEAD_DIM:(r + 1) * HEAD_DIM] = acc / den
            else:
                dst[0][r, dst[3], :] = acc
                dst[1][r, dst[3], :] = m_news[r]
                dst[2][r, dst[3], :] = den

    near, mid, far = DILATIONS
    quarter = seq // mid
    nb_mid = quarter // SPAN
    class_state = (tacc_ref, tm_ref, tl_ref)
    seq_state = (acc_ref, m_ref, l_ref)

    def far_block(e, carry):
        idx = pl.ds(e, SPAN, stride=far)
        dst_idx = pl.ds((e % mid) * quarter + e // mid, SPAN, stride=mid)
        block(idx, idx, cap_one, None, class_state + (dst_idx,))
        return carry

    lax.fori_loop(0, far, far_block, 0, unroll=ATTN_UNROLL)

    def mid_first(e4, carry):
        idx = pl.ds(e4, SPAN, stride=mid)
        src_idx = pl.ds(pl.multiple_of(e4 * quarter, SPAN), SPAN)
        block(idx, idx, cap_one, class_state + (src_idx,), seq_state + (idx,))
        return carry

    lax.fori_loop(0, mid, mid_first, 0, unroll=ATTN_UNROLL)

    def mid_later(t, carry):
        e4 = t // (nb_mid - 1)
        nblk = t % (nb_mid - 1) + 1
        start = e4 + nblk * (mid * SPAN)
        q_idx = pl.ds(start, SPAN, stride=mid)
        k_idx = pl.ds(start - mid * SPAN, 2 * SPAN, stride=mid)
        src_idx = pl.ds(pl.multiple_of(e4 * quarter + nblk * SPAN, SPAN), SPAN)
        block(q_idx, k_idx, cap_two, class_state + (src_idx,), seq_state + (q_idx,))
        return carry

    lax.fori_loop(0, mid * (nb_mid - 1), mid_later, 0, unroll=ATTN_UNROLL)

    first = pl.ds(0, SPAN)
    block(first, first, cap_one, seq_state + (first,), None)

    def near_later(nblk, carry):
        q_idx = pl.ds(pl.multiple_of(nblk * SPAN, SPAN), SPAN)
        k_idx = pl.ds(pl.multiple_of((nblk - 1) * SPAN, SPAN), 2 * SPAN)
        block(q_idx, k_idx, cap_two, seq_state + (q_idx,), None)
        return carry

    lax.fori_loop(1, seq // (near * SPAN), near_later, 0, unroll=ATTN_UNROLL)


def _attention(q, k, v, batch, seq):
    n = q.shape[1]
    assert seq % (DILATIONS[2] * SPAN) == 0
    slab = Q_PER_KV * seq * HEAD_DIM * 4
    vmem = 2 * slab + 2 * 2 * seq * HEAD_DIM * 4 + 2 * slab + 6 * slab
    return pl.pallas_call(
        _attn_kernel,
        grid=(batch, N_KV_HEADS),
        in_specs=[
            pl.BlockSpec((Q_PER_KV, seq, HEAD_DIM), lambda b, g: (g, b, 0)),
            pl.BlockSpec((None, seq, HEAD_DIM), lambda b, g: (g, b, 0)),
            pl.BlockSpec((None, seq, HEAD_DIM), lambda b, g: (g, b, 0)),
        ],
        out_specs=pl.BlockSpec((seq, Q_PER_KV * HEAD_DIM), lambda b, g: (b, g)),
        out_shape=jax.ShapeDtypeStruct((n, Q_DIM), F32),
        scratch_shapes=[pltpu.VMEM((Q_PER_KV, seq, HEAD_DIM), F32)] * 6,
        compiler_params=pltpu.CompilerParams(
            dimension_semantics=("parallel", "parallel"),
            vmem_limit_bytes=min(vmem + (8 << 20), V7X_VMEM_BYTES - (4 << 20))),
        name="attn",
    )(q, k, v)


def _out_proj_kernel(a_ref, cn_ref, x_ref, ga_ref, w_ref, gpost_ref, o_ref):
    an = _rms(a_ref[...], ga_ref[...]).astype(BF16)
    mixed = jnp.dot(an, w_ref[0:Q_DIM, :], preferred_element_type=F32)
    mixed += jnp.dot(cn_ref[...], w_ref[Q_DIM:, :], preferred_element_type=F32)
    o_ref[...] = x_ref[...] + _rms(mixed, gpost_ref[...])


def _out_proj(a, cn, x, ga, w_out, gpost, layer, *, tm=512):
    n, d = x.shape
    assert n % tm == 0
    vmem = (2 * tm * Q_DIM * 4 + 2 * tm * CONV_DIM * 2 + 4 * tm * d * 4
            + 2 * (Q_DIM + CONV_DIM) * d * 2 + 2 * tm * d * 4)
    return pl.pallas_call(
        _out_proj_kernel,
        grid=(n // tm,),
        in_specs=[
            pl.BlockSpec((tm, Q_DIM), lambda i: (i, 0)),
            pl.BlockSpec((tm, CONV_DIM), lambda i: (i, 0)),
            pl.BlockSpec((tm, d), lambda i: (i, 0)),
            pl.BlockSpec((None, 1, Q_DIM), lambda i: (layer, 0, 0)),
            pl.BlockSpec((None, Q_DIM + CONV_DIM, d), lambda i: (layer, 0, 0)),
            pl.BlockSpec((None, 1, d), lambda i: (layer, 0, 0)),
        ],
        out_specs=pl.BlockSpec((tm, d), lambda i: (i, 0)),
        out_shape=jax.ShapeDtypeStruct((n, d), F32),
        compiler_params=pltpu.CompilerParams(
            dimension_semantics=("parallel",),
            vmem_limit_bytes=min(vmem + (4 << 20), V7X_VMEM_BYTES - (4 << 20))),
        name="out_proj",
    )(a, cn, x, ga, w_out, gpost)


def kernel(x, ffn1_norm_pre, ffn1_w_gate_up, ffn1_w_down, ffn1_norm_post, mix_norm_pre, w_in, conv_w, attn_out_norm, conv_out_norm, w_out, mix_norm_post, ffn2_norm_pre, ffn2_w_gate_up, ffn2_w_down, ffn2_norm_post):
    batch, seq, d = x.shape
    depth = w_in.shape[0]
    xf = x.reshape(batch * seq, d)

    def row(gain):
        return gain.reshape(gain.shape[0], 1, gain.shape[1])

    nj = ffn1_w_down.shape[1] // FFN_HIDDEN_TILE
    w1 = (ffn1_w_gate_up[0].astype(BF16), ffn1_w_down[0].astype(BF16))
    for l in range(depth):
        casts = [_cast_rows(ffn2_w_gate_up, l, GATE_UP_CAST_ROWS, nj),
                 _cast_rows(ffn2_w_down, l, CAST_ROWS, nj)]
        if l == 0:
            casts += [_cast_rows(w_in, None, CAST_ROWS, nj), _cast_rows(w_out, None, CAST_ROWS, nj)]
        xf, done = _ffn(xf, row(ffn1_norm_pre), w1[0], w1[1], row(ffn1_norm_post), l, casts)
        w2 = done[:2]
        if l == 0:
            w_in_b = done[2].reshape(w_in.shape)
            w_out_b = done[3].reshape(w_out.shape)
        q, k, v, cn = _in_proj(xf, row(mix_norm_pre), w_in_b, conv_w, row(conv_out_norm), l, seq)
        a = _attention(q, k, v, batch, seq)
        xf = _out_proj(a, cn, xf, row(attn_out_norm), w_out_b, row(mix_norm_post), l)
        casts = []
        if l + 1 < depth:
            casts = [_cast_rows(ffn1_w_gate_up, l + 1, GATE_UP_CAST_ROWS, nj),
                     _cast_rows(ffn1_w_down, l + 1, CAST_ROWS, nj)]
        xf, w1 = _ffn(xf, row(ffn2_norm_pre), w2[0], w2[1], row(ffn2_norm_post), l, casts)
    return xf.reshape(batch, seq, d)
```

```python
import functools
from typing import Callable, NamedTuple

import jax
import jax.numpy as jnp
from jax import lax
from jax.experimental import pallas as pl
from jax.experimental.pallas import tpu as pltpu

D_MODEL = 2048
HEAD_DIM = 128
ATTN_WIDTH = D_MODEL // 2
CONV_DIM = D_MODEL - ATTN_WIDTH
N_Q_HEADS = ATTN_WIDTH // HEAD_DIM
N_KV_HEADS = max(1, N_Q_HEADS // 4)
Q_PER_KV = N_Q_HEADS // N_KV_HEADS
Q_DIM = N_Q_HEADS * HEAD_DIM
KV_DIM = N_KV_HEADS * HEAD_DIM
CONV_WIDTH = 3
IN_PROJ_DIM = Q_DIM + 2 * KV_DIM + 3 * CONV_DIM
SPAN = 128
DILATIONS = (1, 4, 16)
FFN_RESIDUAL_WEIGHT = 0.5
NORM_EPS = 1e-6
LOG2_E = 1.4426950408889634

V7X_VMEM_BYTES = 64 * 1024 * 1024
SUBLANES = 8
FFN_TOKEN_TILE = 1024
FFN_HIDDEN_TILE = 512
FFN_ROW_SUBTILE = 512
PROJ_ROW_SUBTILE = 256
OUT_ROW_SUBTILE = 512
CAST_ROWS = 32
GATE_UP_CAST_ROWS = 16
NORM_ROW_CHUNK = 128
ATTN_UNROLL = 16

BF16 = jnp.bfloat16
F32 = jnp.float32

_OFF_K = Q_DIM
_OFF_V = Q_DIM + KV_DIM
_OFF_H = Q_DIM + 2 * KV_DIM
_OFF_B = _OFF_H + CONV_DIM
_OFF_C = _OFF_B + CONV_DIM


def _rms(x, gain):
    y = x * lax.rsqrt(jnp.mean(x * x, axis=-1, keepdims=True) + NORM_EPS)
    return y * gain


def _ffn_kernel(*refs, n_casts):
    x_ref, gpre_ref, wg_ref, wu_ref, wd_ref, gpost_ref = refs[:6]
    cast_src = refs[6:6 + n_casts]
    o_ref = refs[6 + n_casts]
    cast_dst = refs[7 + n_casts:7 + 2 * n_casts]
    h_ref = refs[7 + 2 * n_casts]
    j = pl.program_id(1)
    last = pl.num_programs(1) - 1
    n_rows = h_ref.shape[0]

    def step(first_step, last_step):
        for src, dst in zip(cast_src, cast_dst):
            dst[...] = src[...].astype(BF16)
        if last_step:
            half_gain = FFN_RESIDUAL_WEIGHT * gpost_ref[...]
        for r0 in range(0, n_rows, FFN_ROW_SUBTILE):
            if first_step:
                for c0 in range(r0, r0 + FFN_ROW_SUBTILE, NORM_ROW_CHUNK):
                    rows = slice(c0, c0 + NORM_ROW_CHUNK)
                    h_ref[rows, :] = _rms(x_ref[rows, :], gpre_ref[...]).astype(BF16)
            rows = slice(r0, r0 + FFN_ROW_SUBTILE)
            h = h_ref[rows, :]
            g = jnp.dot(h, wg_ref[...], preferred_element_type=F32)
            u = jnp.dot(h, wu_ref[...], preferred_element_type=F32)
            a = (g * jax.nn.sigmoid(g) * u).astype(BF16)
            part = jnp.dot(a, wd_ref[...], preferred_element_type=F32)
            if first_step:
                o_ref[rows, :] = part
            else:
                o_ref[rows, :] += part
            if last_step:
                for c0 in range(r0, r0 + FFN_ROW_SUBTILE, NORM_ROW_CHUNK):
                    rows = slice(c0, c0 + NORM_ROW_CHUNK)
                    o_ref[rows, :] = x_ref[rows, :] + _rms(o_ref[rows, :], half_gain)

    pl.when(j == 0)(lambda: step(True, False))
    pl.when((j > 0) & (j < last))(lambda: step(False, False))
    pl.when(j == last)(lambda: step(False, True))


class _Cast(NamedTuple):
    src: jax.Array
    src_block: tuple
    src_map: Callable
    out_shape: tuple
    out_block: tuple
    out_map: Callable


def _cast_rows(w, layer, block_rows, nj):
    depth, rows, cols = w.shape
    if layer is None:
        w, rows = w.reshape(depth * rows, cols), depth * rows
    assert rows % block_rows == 0
    last_block = rows // block_rows - 1

    def out_map(i, j):
        return (jnp.minimum(i * nj + j, last_block), 0)

    if layer is None:
        return _Cast(w, (block_rows, cols), out_map, (rows, cols), (block_rows, cols), out_map)
    return _Cast(w, (None, block_rows, cols), lambda i, j: (layer,) + out_map(i, j),
                 (rows, cols), (block_rows, cols), out_map)


def _ffn(x, gpre, w_gu, w_down, gpost, layer, casts=()):
    n, d = x.shape
    d_ff = w_down.shape[0]
    tm, tf = FFN_TOKEN_TILE, FFN_HIDDEN_TILE
    nj = d_ff // tf
    assert n % tm == 0 and d_ff % tf == 0
    assert all(c.out_shape[0] // c.out_block[0] <= (n // tm) * nj for c in casts)
    cast_bytes = sum(2 * (4 + 2) * c.out_block[0] * c.out_block[1] for c in casts)
    vmem = (2 * 2 * tm * d * 4
            + tm * d * 2
            + 2 * 3 * d * tf * 2
            + 6 * FFN_ROW_SUBTILE * tf * 4
            + cast_bytes)
    outs = pl.pallas_call(
        functools.partial(_ffn_kernel, n_casts=len(casts)),
        grid=(n // tm, nj),
        in_specs=[
            pl.BlockSpec((tm, d), lambda i, j: (i, 0)),
            pl.BlockSpec((None, 1, d), lambda i, j: (layer, 0, 0)),
            pl.BlockSpec((d, tf), lambda i, j: (0, j)),
            pl.BlockSpec((d, tf), lambda i, j: (0, j + nj)),
            pl.BlockSpec((tf, d), lambda i, j: (j, 0)),
            pl.BlockSpec((None, 1, d), lambda i, j: (layer, 0, 0)),
        ] + [pl.BlockSpec(c.src_block, c.src_map) for c in casts],
        out_specs=[pl.BlockSpec((tm, d), lambda i, j: (i, 0))]
        + [pl.BlockSpec(c.out_block, c.out_map) for c in casts],
        out_shape=[jax.ShapeDtypeStruct((n, d), F32)]
        + [jax.ShapeDtypeStruct(c.out_shape, BF16) for c in casts],
        scratch_shapes=[pltpu.VMEM((tm, d), BF16)],
        compiler_params=pltpu.CompilerParams(
            dimension_semantics=("arbitrary", "arbitrary"),
            vmem_limit_bytes=min(vmem + (4 << 20), V7X_VMEM_BYTES - (4 << 20))),
        name="ffn",
    )(x, gpre, w_gu, w_gu, w_down, gpost, *[c.src for c in casts])
    return outs[0], outs[1:]


def _in_proj_kernel(x_ref, gpre_ref, w_ref, cw_ref, gc_ref, q_ref, k_ref, v_ref, cn_ref,
                    u_ref, *, tiles_per_seq):
    i = pl.program_id(0)
    tm = x_ref.shape[0]

    @pl.when(i % tiles_per_seq == 0)
    def _():
        u_ref[0:SUBLANES, :] = jnp.zeros((SUBLANES, CONV_DIM), F32)

    @pl.when(i % tiles_per_seq != 0)
    def _():
        u_ref[0:SUBLANES, :] = u_ref[tm:tm + SUBLANES, :]

    for r0 in range(0, tm, PROJ_ROW_SUBTILE):
        rows = slice(r0, r0 + PROJ_ROW_SUBTILE)
        h = _rms(x_ref[rows, :], gpre_ref[...]).astype(BF16)

        def proj(off, width):
            return jnp.dot(h, w_ref[:, off:off + width], preferred_element_type=F32)

        base = SUBLANES + r0
        u_ref[base:base + PROJ_ROW_SUBTILE, :] = proj(_OFF_C, CONV_DIM) * proj(_OFF_H, CONV_DIM)
        y = (cw_ref[2:3, :] * u_ref[base:base + PROJ_ROW_SUBTILE, :]
             + cw_ref[1:2, :] * u_ref[base - 1:base - 1 + PROJ_ROW_SUBTILE, :]
             + cw_ref[0:1, :] * u_ref[base - 2:base - 2 + PROJ_ROW_SUBTILE, :])
        c = proj(_OFF_B, CONV_DIM) * y
        cn_ref[rows, :] = _rms(c, gc_ref[...]).astype(BF16)

        zq = proj(0, Q_DIM)
        for hd in range(N_Q_HEADS):
            q_ref[hd, rows, :] = zq[:, hd * HEAD_DIM:(hd + 1) * HEAD_DIM]
        zk = proj(_OFF_K, KV_DIM)
        zv = proj(_OFF_V, KV_DIM)
        for hd in range(N_KV_HEADS):
            k_ref[hd, rows, :] = zk[:, hd * HEAD_DIM:(hd + 1) * HEAD_DIM]
            v_ref[hd, rows, :] = zv[:, hd * HEAD_DIM:(hd + 1) * HEAD_DIM]


def _in_proj(x, gpre, w_in, conv_w, gconv, layer, seq, *, tm=512):
    n, d = x.shape
    assert n % tm == 0 and seq % tm == 0
    vmem = (2 * tm * d * 4 + 2 * d * IN_PROJ_DIM * 2
            + 2 * tm * (Q_DIM + 2 * KV_DIM) * 4 + 2 * tm * CONV_DIM * 2
            + (tm + SUBLANES) * CONV_DIM * 4 + 4 * tm * CONV_DIM * 4)
    return pl.pallas_call(
        functools.partial(_in_proj_kernel, tiles_per_seq=seq // tm),
        grid=(n // tm,),
        in_specs=[
            pl.BlockSpec((tm, d), lambda i: (i, 0)),
            pl.BlockSpec((None, 1, d), lambda i: (layer, 0, 0)),
            pl.BlockSpec((None, d, IN_PROJ_DIM), lambda i: (layer, 0, 0)),
            pl.BlockSpec((None, CONV_WIDTH, CONV_DIM), lambda i: (layer, 0, 0)),
            pl.BlockSpec((None, 1, CONV_DIM), lambda i: (layer, 0, 0)),
        ],
        out_specs=[
            pl.BlockSpec((N_Q_HEADS, tm, HEAD_DIM), lambda i: (0, i, 0)),
            pl.BlockSpec((N_KV_HEADS, tm, HEAD_DIM), lambda i: (0, i, 0)),
            pl.BlockSpec((N_KV_HEADS, tm, HEAD_DIM), lambda i: (0, i, 0)),
            pl.BlockSpec((tm, CONV_DIM), lambda i: (i, 0)),
        ],
        out_shape=[
            jax.ShapeDtypeStruct((N_Q_HEADS, n, HEAD_DIM), F32),
            jax.ShapeDtypeStruct((N_KV_HEADS, n, HEAD_DIM), F32),
            jax.ShapeDtypeStruct((N_KV_HEADS, n, HEAD_DIM), F32),
            jax.ShapeDtypeStruct((n, CONV_DIM), BF16),
        ],
        scratch_shapes=[pltpu.VMEM((tm + SUBLANES, CONV_DIM), F32)],
        compiler_params=pltpu.CompilerParams(
            dimension_semantics=("arbitrary",),
            vmem_limit_bytes=min(vmem + (4 << 20), V7X_VMEM_BYTES - (4 << 20))),
        name="in_proj",
    )(x, gpre, w_in, conv_w, gconv)


def _attn_kernel(q_ref, k_ref, v_ref, o_ref, acc_ref, m_ref, l_ref, tacc_ref, tm_ref, tl_ref):
    seq = k_ref.shape[0]
    logit_scale = (HEAD_DIM ** -0.5) * LOG2_E
    rows = Q_PER_KV * SPAN
    qi = lax.broadcasted_iota(jnp.int32, (rows, 2 * SPAN), 0) % SPAN
    kj = lax.broadcasted_iota(jnp.int32, (rows, 2 * SPAN), 1)
    cap_two = jnp.where((kj >= qi) & (kj <= qi + SPAN), jnp.inf, -jnp.inf)
    qi1 = lax.broadcasted_iota(jnp.int32, (rows, SPAN), 0) % SPAN
    kj1 = lax.broadcasted_iota(jnp.int32, (rows, SPAN), 1)
    cap_one = jnp.where(kj1 <= qi1, jnp.inf, -jnp.inf)

    def block(q_idx, k_idx, cap, src, dst):
        width = cap.shape[1]
        q = jnp.concatenate([q_ref[r, q_idx, :] for r in range(Q_PER_KV)], axis=0).astype(BF16)
        k = k_ref[k_idx, :].astype(BF16)
        v = v_ref[k_idx, :].astype(BF16)
        v_ones = jnp.concatenate([v, jnp.ones((width, HEAD_DIM), BF16)], axis=1)
        t = lax.dot_general(q, k, (((1,), (1,)), ((), ())), preferred_element_type=F32)
        t = jnp.minimum(t * logit_scale, cap)
        ps, m_news, a_olds = [], [], []
        for r in range(Q_PER_KV):
            t_r = t[r * SPAN:(r + 1) * SPAN]
            m_blk = jnp.broadcast_to(jnp.max(t_r, axis=-1, keepdims=True), (SPAN, HEAD_DIM))
            if src is None:
                m_new = m_blk
            else:
                m_old = src[1][r, src[3], :]
                m_new = jnp.maximum(m_old, m_blk)
                a_olds.append(jnp.exp2(m_old - m_new))
            m_news.append(m_new)
            p_r = [jnp.exp2(t_r[:, c * HEAD_DIM:(c + 1) * HEAD_DIM] - m_new)
                   for c in range(width // HEAD_DIM)]
            ps.append(jnp.concatenate(p_r, axis=1).astype(BF16))
        o = jnp.dot(jnp.concatenate(ps, axis=0), v_ones, preferred_element_type=F32)
        for r in range(Q_PER_KV):
            acc = o[r * SPAN:(r + 1) * SPAN, :HEAD_DIM]
            den = o[r * SPAN:(r + 1) * SPAN, HEAD_DIM:]
            if src is not None:
                acc = src[0][r, src[3], :] * a_olds[r] + acc
                den = src[2][r, src[3], :] * a_olds[r] + den
            if dst is None:
                o_ref[q_idx, r * HEAD_DIM:(r + 1) * HEAD_DIM] = acc / den
            else:
                dst[0][r, dst[3], :] = acc
                dst[1][r, dst[3], :] = m_news[r]
                dst[2][r, dst[3], :] = den

    near, mid, far = DILATIONS
    quarter = seq // mid
    nb_mid = quarter // SPAN
    class_state = (tacc_ref, tm_ref, tl_ref)
    seq_state = (acc_ref, m_ref, l_ref)

    def far_block(e, carry):
        idx = pl.ds(e, SPAN, stride=far)
        dst_idx = pl.ds((e % mid) * quarter + e // mid, SPAN, stride=mid)
        block(idx, idx, cap_one, None, class_state + (dst_idx,))
        return carry

    lax.fori_loop(0, far, far_block, 0, unroll=ATTN_UNROLL)

    def mid_first(e4, carry):
        idx = pl.ds(e4, SPAN, stride=mid)
        src_idx = pl.ds(pl.multiple_of(e4 * quarter, SPAN), SPAN)
        block(idx, idx, cap_one, class_state + (src_idx,), seq_state + (idx,))
        return carry

    lax.fori_loop(0, mid, mid_first, 0, unroll=ATTN_UNROLL)

    def mid_later(t, carry):
        e4 = t // (nb_mid - 1)
        nblk = t % (nb_mid - 1) + 1
        start = e4 + nblk * (mid * SPAN)
        q_idx = pl.ds(start, SPAN, stride=mid)
        k_idx = pl.ds(start - mid * SPAN, 2 * SPAN, stride=mid)
        src_idx = pl.ds(pl.multiple_of(e4 * quarter + nblk * SPAN, SPAN), SPAN)
        block(q_idx, k_idx, cap_two, class_state + (src_idx,), seq_state + (q_idx,))
        return carry

    lax.fori_loop(0, mid * (nb_mid - 1), mid_later, 0, unroll=ATTN_UNROLL)

    first = pl.ds(0, SPAN)
    block(first, first, cap_one, seq_state + (first,), None)

    def near_later(nblk, carry):
        q_idx = pl.ds(pl.multiple_of(nblk * SPAN, SPAN), SPAN)
        k_idx = pl.ds(pl.multiple_of((nblk - 1) * SPAN, SPAN), 2 * SPAN)
        block(q_idx, k_idx, cap_two, seq_state + (q_idx,), None)
        return carry

    lax.fori_loop(1, seq // (near * SPAN), near_later, 0, unroll=ATTN_UNROLL)


def _attention(q, k, v, batch, seq):
    n = q.shape[1]
    assert seq % (DILATIONS[2] * SPAN) == 0
    slab = Q_PER_KV * seq * HEAD_DIM * 4
    vmem = 2 * slab + 2 * 2 * seq * HEAD_DIM * 4 + 2 * slab + 6 * slab
    return pl.pallas_call(
        _attn_kernel,
        grid=(batch, N_KV_HEADS),
        in_specs=[
            pl.BlockSpec((Q_PER_KV, seq, HEAD_DIM), lambda b, g: (g, b, 0)),
            pl.BlockSpec((None, seq, HEAD_DIM), lambda b, g: (g, b, 0)),
            pl.BlockSpec((None, seq, HEAD_DIM), lambda b, g: (g, b, 0)),
        ],
        out_specs=pl.BlockSpec((seq, Q_PER_KV * HEAD_DIM), lambda b, g: (b, g)),
        out_shape=jax.ShapeDtypeStruct((n, Q_DIM), F32),
        scratch_shapes=[pltpu.VMEM((Q_PER_KV, seq, HEAD_DIM), F32)] * 6,
        compiler_params=pltpu.CompilerParams(
            dimension_semantics=("parallel", "parallel"),
            vmem_limit_bytes=min(vmem + (8 << 20), V7X_VMEM_BYTES - (4 << 20))),
        name="attn",
    )(q, k, v)


def _out_proj_kernel(a_ref, cn_ref, x_ref, ga_ref, w_ref, gpost_ref, o_ref):
    for r0 in range(0, x_ref.shape[0], OUT_ROW_SUBTILE):
        rows = slice(r0, r0 + OUT_ROW_SUBTILE)
        an = _rms(a_ref[rows, :], ga_ref[...]).astype(BF16)
        mixed = jnp.dot(an, w_ref[0:Q_DIM, :], preferred_element_type=F32)
        mixed += jnp.dot(cn_ref[rows, :], w_ref[Q_DIM:, :], preferred_element_type=F32)
        o_ref[rows, :] = x_ref[rows, :] + _rms(mixed, gpost_ref[...])


def _out_proj(a, cn, x, ga, w_out, gpost, layer, *, tm=1024):
    n, d = x.shape
    assert n % tm == 0 and tm % OUT_ROW_SUBTILE == 0
    vmem = (2 * tm * Q_DIM * 4 + 2 * tm * CONV_DIM * 2 + 4 * tm * d * 4
            + (Q_DIM + CONV_DIM) * d * 2 + 2 * OUT_ROW_SUBTILE * d * 4)
    return pl.pallas_call(
        _out_proj_kernel,
        grid=(n // tm,),
        in_specs=[
            pl.BlockSpec((tm, Q_DIM), lambda i: (i, 0)),
            pl.BlockSpec((tm, CONV_DIM), lambda i: (i, 0)),
            pl.BlockSpec((tm, d), lambda i: (i, 0)),
            pl.BlockSpec((None, 1, Q_DIM), lambda i: (layer, 0, 0)),
            pl.BlockSpec((None, Q_DIM + CONV_DIM, d), lambda i: (layer, 0, 0),
                         pipeline_mode=pl.Buffered(1)),
            pl.BlockSpec((None, 1, d), lambda i: (layer, 0, 0)),
        ],
        out_specs=pl.BlockSpec((tm, d), lambda i: (i, 0)),
        out_shape=jax.ShapeDtypeStruct((n, d), F32),
        compiler_params=pltpu.CompilerParams(
            dimension_semantics=("parallel",),
            vmem_limit_bytes=min(vmem + (4 << 20), V7X_VMEM_BYTES - (4 << 20))),
        name="out_proj",
    )(a, cn, x, ga, w_out, gpost)


def kernel(x, ffn1_norm_pre, ffn1_w_gate_up, ffn1_w_down, ffn1_norm_post, mix_norm_pre, w_in, conv_w, attn_out_norm, conv_out_norm, w_out, mix_norm_post, ffn2_norm_pre, ffn2_w_gate_up, ffn2_w_down, ffn2_norm_post):
    batch, seq, d = x.shape
    depth = w_in.shape[0]
    xf = x.reshape(batch * seq, d)

    def row(gain):
        return gain.reshape(gain.shape[0], 1, gain.shape[1])

    nj = ffn1_w_down.shape[1] // FFN_HIDDEN_TILE
    w1 = (ffn1_w_gate_up[0].astype(BF16), ffn1_w_down[0].astype(BF16))
    for l in range(depth):
        casts = [_cast_rows(ffn2_w_gate_up, l, GATE_UP_CAST_ROWS, nj),
                 _cast_rows(ffn2_w_down, l, CAST_ROWS, nj)]
        if l == 0:
            casts += [_cast_rows(w_in, None, CAST_ROWS, nj), _cast_rows(w_out, None, CAST_ROWS, nj)]
        xf, done = _ffn(xf, row(ffn1_norm_pre), w1[0], w1[1], row(ffn1_norm_post), l, casts)
        w2 = done[:2]
        if l == 0:
            w_in_b = done[2].reshape(w_in.shape)
            w_out_b = done[3].reshape(w_out.shape)
        q, k, v, cn = _in_proj(xf, row(mix_norm_pre), w_in_b, conv_w, row(conv_out_norm), l, seq)
        a = _attention(q, k, v, batch, seq)
        xf = _out_proj(a, cn, xf, row(attn_out_norm), w_out_b, row(mix_norm_post), l)
        casts = []
        if l + 1 < depth:
            casts = [_cast_rows(ffn1_w_gate_up, l + 1, GATE_UP_CAST_ROWS, nj),
                     _cast_rows(ffn1_w_down, l + 1, CAST_ROWS, nj)]
        xf, w1 = _ffn(xf, row(ffn2_norm_pre), w2[0], w2[1], row(ffn2_norm_post), l, casts)
    return xf.reshape(batch, seq, d)
```

```python
import functools
from typing import Callable, NamedTuple

import jax
import jax.numpy as jnp
from jax import lax
from jax.experimental import pallas as pl
from jax.experimental.pallas import tpu as pltpu

D_MODEL = 2048
HEAD_DIM = 128
ATTN_WIDTH = D_MODEL // 2
CONV_DIM = D_MODEL - ATTN_WIDTH
N_Q_HEADS = ATTN_WIDTH // HEAD_DIM
N_KV_HEADS = max(1, N_Q_HEADS // 4)
Q_PER_KV = N_Q_HEADS // N_KV_HEADS
Q_DIM = N_Q_HEADS * HEAD_DIM
KV_DIM = N_KV_HEADS * HEAD_DIM
CONV_WIDTH = 3
IN_PROJ_DIM = Q_DIM + 2 * KV_DIM + 3 * CONV_DIM
SPAN = 128
DILATIONS = (1, 4, 16)
FFN_RESIDUAL_WEIGHT = 0.5
NORM_EPS = 1e-6
LOG2_E = 1.4426950408889634

V7X_VMEM_BYTES = 64 * 1024 * 1024
SUBLANES = 8
FFN_TOKEN_TILE = 1024
FFN_HIDDEN_TILE = 512
FFN_ROW_SUBTILE = 512
PROJ_ROW_SUBTILE = 256
CAST_ROWS = 32
GATE_UP_CAST_ROWS = 16
NORM_ROW_CHUNK = 128
ATTN_UNROLL = 16

BF16 = jnp.bfloat16
F32 = jnp.float32

_OFF_K = Q_DIM
_OFF_V = Q_DIM + KV_DIM
_OFF_H = Q_DIM + 2 * KV_DIM
_OFF_B = _OFF_H + CONV_DIM
_OFF_C = _OFF_B + CONV_DIM


def _rms(x, gain):
    y = x * lax.rsqrt(jnp.mean(x * x, axis=-1, keepdims=True) + NORM_EPS)
    return y * gain


def _ffn_kernel(*refs, n_casts):
    x_ref, gpre_ref, wg_ref, wu_ref, wd_ref, gpost_ref = refs[:6]
    cast_src = refs[6:6 + n_casts]
    o_ref = refs[6 + n_casts]
    cast_dst = refs[7 + n_casts:7 + 2 * n_casts]
    h_ref = refs[7 + 2 * n_casts]
    j = pl.program_id(1)
    last = pl.num_programs(1) - 1
    n_rows = h_ref.shape[0]

    def step(first_step, last_step):
        for src, dst in zip(cast_src, cast_dst):
            dst[...] = src[...].astype(BF16)
        if last_step:
            half_gain = FFN_RESIDUAL_WEIGHT * gpost_ref[...]
        for r0 in range(0, n_rows, FFN_ROW_SUBTILE):
            if first_step:
                for c0 in range(r0, r0 + FFN_ROW_SUBTILE, NORM_ROW_CHUNK):
                    rows = slice(c0, c0 + NORM_ROW_CHUNK)
                    h_ref[rows, :] = _rms(x_ref[rows, :], gpre_ref[...]).astype(BF16)
            rows = slice(r0, r0 + FFN_ROW_SUBTILE)
            h = h_ref[rows, :]
            g = jnp.dot(h, wg_ref[...], preferred_element_type=F32)
            u = jnp.dot(h, wu_ref[...], preferred_element_type=F32)
            a = (g * jax.nn.sigmoid(g) * u).astype(BF16)
            part = jnp.dot(a, wd_ref[...], preferred_element_type=F32)
            if first_step:
                o_ref[rows, :] = part
            else:
                o_ref[rows, :] += part
            if last_step:
                for c0 in range(r0, r0 + FFN_ROW_SUBTILE, NORM_ROW_CHUNK):
                    rows = slice(c0, c0 + NORM_ROW_CHUNK)
                    o_ref[rows, :] = x_ref[rows, :] + _rms(o_ref[rows, :], half_gain)

    pl.when(j == 0)(lambda: step(True, False))
    pl.when((j > 0) & (j < last))(lambda: step(False, False))
    pl.when(j == last)(lambda: step(False, True))


class _Cast(NamedTuple):
    src: jax.Array
    src_block: tuple
    src_map: Callable
    out_shape: tuple
    out_block: tuple
    out_map: Callable


def _cast_rows(w, layer, block_rows, nj):
    depth, rows, cols = w.shape
    if layer is None:
        w, rows = w.reshape(depth * rows, cols), depth * rows
    assert rows % block_rows == 0
    last_block = rows // block_rows - 1

    def out_map(i, j):
        return (jnp.minimum(i * nj + j, last_block), 0)

    if layer is None:
        return _Cast(w, (block_rows, cols), out_map, (rows, cols), (block_rows, cols), out_map)
    return _Cast(w, (None, block_rows, cols), lambda i, j: (layer,) + out_map(i, j),
                 (rows, cols), (block_rows, cols), out_map)


def _ffn(x, gpre, w_gu, w_down, gpost, layer, casts=()):
    n, d = x.shape
    d_ff = w_down.shape[0]
    tm, tf = FFN_TOKEN_TILE, FFN_HIDDEN_TILE
    nj = d_ff // tf
    assert n % tm == 0 and d_ff % tf == 0 and tm % FFN_ROW_SUBTILE == 0
    assert nj >= 2, "the first and the last grid step must be different steps"
    assert all(c.out_shape[0] // c.out_block[0] <= (n // tm) * nj for c in casts)
    cast_bytes = sum(2 * (4 + 2) * c.out_block[0] * c.out_block[1] for c in casts)
    vmem = (2 * 2 * tm * d * 4
            + tm * d * 2
            + 2 * 3 * d * tf * 2
            + 6 * FFN_ROW_SUBTILE * tf * 4
            + cast_bytes)
    outs = pl.pallas_call(
        functools.partial(_ffn_kernel, n_casts=len(casts)),
        grid=(n // tm, nj),
        in_specs=[
            pl.BlockSpec((tm, d), lambda i, j: (i, 0)),
            pl.BlockSpec((None, 1, d), lambda i, j: (layer, 0, 0)),
            pl.BlockSpec((d, tf), lambda i, j: (0, j)),
            pl.BlockSpec((d, tf), lambda i, j: (0, j + nj)),
            pl.BlockSpec((tf, d), lambda i, j: (j, 0)),
            pl.BlockSpec((None, 1, d), lambda i, j: (layer, 0, 0)),
        ] + [pl.BlockSpec(c.src_block, c.src_map) for c in casts],
        out_specs=[pl.BlockSpec((tm, d), lambda i, j: (i, 0))]
        + [pl.BlockSpec(c.out_block, c.out_map) for c in casts],
        out_shape=[jax.ShapeDtypeStruct((n, d), F32)]
        + [jax.ShapeDtypeStruct(c.out_shape, BF16) for c in casts],
        scratch_shapes=[pltpu.VMEM((tm, d), BF16)],
        compiler_params=pltpu.CompilerParams(
            dimension_semantics=("arbitrary", "arbitrary"),
            vmem_limit_bytes=min(vmem + (4 << 20), V7X_VMEM_BYTES - (4 << 20))),
        name="ffn",
    )(x, gpre, w_gu, w_gu, w_down, gpost, *[c.src for c in casts])
    return outs[0], outs[1:]


def _in_proj_kernel(x_ref, gpre_ref, w_ref, cw_ref, gc_ref, q_ref, k_ref, v_ref, cn_ref,
                    u_ref, *, tiles_per_seq):
    i = pl.program_id(0)
    tm = x_ref.shape[0]

    @pl.when(i % tiles_per_seq == 0)
    def _():
        u_ref[0:SUBLANES, :] = jnp.zeros((SUBLANES, CONV_DIM), F32)

    @pl.when(i % tiles_per_seq != 0)
    def _():
        u_ref[0:SUBLANES, :] = u_ref[tm:tm + SUBLANES, :]

    for r0 in range(0, tm, PROJ_ROW_SUBTILE):
        rows = slice(r0, r0 + PROJ_ROW_SUBTILE)
        h = _rms(x_ref[rows, :], gpre_ref[...]).astype(BF16)

        def proj(off, width):
            return jnp.dot(h, w_ref[:, off:off + width], preferred_element_type=F32)

        base = SUBLANES + r0
        u_ref[base:base + PROJ_ROW_SUBTILE, :] = proj(_OFF_C, CONV_DIM) * proj(_OFF_H, CONV_DIM)
        y = (cw_ref[2:3, :] * u_ref[base:base + PROJ_ROW_SUBTILE, :]
             + cw_ref[1:2, :] * u_ref[base - 1:base - 1 + PROJ_ROW_SUBTILE, :]
             + cw_ref[0:1, :] * u_ref[base - 2:base - 2 + PROJ_ROW_SUBTILE, :])
        c = proj(_OFF_B, CONV_DIM) * y
        cn_ref[rows, :] = _rms(c, gc_ref[...]).astype(BF16)

        zq = proj(0, Q_DIM)
        for hd in range(N_Q_HEADS):
            q_ref[hd, rows, :] = zq[:, hd * HEAD_DIM:(hd + 1) * HEAD_DIM]
        zk = proj(_OFF_K, KV_DIM)
        zv = proj(_OFF_V, KV_DIM)
        for hd in range(N_KV_HEADS):
            k_ref[hd, rows, :] = zk[:, hd * HEAD_DIM:(hd + 1) * HEAD_DIM]
            v_ref[hd, rows, :] = zv[:, hd * HEAD_DIM:(hd + 1) * HEAD_DIM]


def _in_proj(x, gpre, w_in, conv_w, gconv, layer, seq, *, tm=512):
    n, d = x.shape
    assert n % tm == 0 and seq % tm == 0 and tm % PROJ_ROW_SUBTILE == 0
    vmem = (2 * tm * d * 4 + 2 * d * IN_PROJ_DIM * 2
            + 2 * tm * (Q_DIM + 2 * KV_DIM) * 4 + 2 * tm * CONV_DIM * 2
            + (tm + SUBLANES) * CONV_DIM * 4 + 4 * tm * CONV_DIM * 4)
    return pl.pallas_call(
        functools.partial(_in_proj_kernel, tiles_per_seq=seq // tm),
        grid=(n // tm,),
        in_specs=[
            pl.BlockSpec((tm, d), lambda i: (i, 0)),
            pl.BlockSpec((None, 1, d), lambda i: (layer, 0, 0)),
            pl.BlockSpec((None, d, IN_PROJ_DIM), lambda i: (layer, 0, 0)),
            pl.BlockSpec((None, CONV_WIDTH, CONV_DIM), lambda i: (layer, 0, 0)),
            pl.BlockSpec((None, 1, CONV_DIM), lambda i: (layer, 0, 0)),
        ],
        out_specs=[
            pl.BlockSpec((N_Q_HEADS, tm, HEAD_DIM), lambda i: (0, i, 0)),
            pl.BlockSpec((N_KV_HEADS, tm, HEAD_DIM), lambda i: (0, i, 0)),
            pl.BlockSpec((N_KV_HEADS, tm, HEAD_DIM), lambda i: (0, i, 0)),
            pl.BlockSpec((tm, CONV_DIM), lambda i: (i, 0)),
        ],
        out_shape=[
            jax.ShapeDtypeStruct((N_Q_HEADS, n, HEAD_DIM), F32),
            jax.ShapeDtypeStruct((N_KV_HEADS, n, HEAD_DIM), F32),
            jax.ShapeDtypeStruct((N_KV_HEADS, n, HEAD_DIM), F32),
            jax.ShapeDtypeStruct((n, CONV_DIM), BF16),
        ],
        scratch_shapes=[pltpu.VMEM((tm + SUBLANES, CONV_DIM), F32)],
        compiler_params=pltpu.CompilerParams(
            dimension_semantics=("arbitrary",),
            vmem_limit_bytes=min(vmem + (4 << 20), V7X_VMEM_BYTES - (4 << 20))),
        name="in_proj",
    )(x, gpre, w_in, conv_w, gconv)


def _attn_kernel(q_ref, k_ref, v_ref, o_ref, acc_ref, m_ref, l_ref, tacc_ref, tm_ref, tl_ref):
    seq = k_ref.shape[0]
    logit_scale = (HEAD_DIM ** -0.5) * LOG2_E
    rows = Q_PER_KV * SPAN
    qi = lax.broadcasted_iota(jnp.int32, (rows, 2 * SPAN), 0) % SPAN
    kj = lax.broadcasted_iota(jnp.int32, (rows, 2 * SPAN), 1)
    cap_two = jnp.where((kj >= qi) & (kj <= qi + SPAN), jnp.inf, -jnp.inf)
    qi1 = lax.broadcasted_iota(jnp.int32, (rows, SPAN), 0) % SPAN
    kj1 = lax.broadcasted_iota(jnp.int32, (rows, SPAN), 1)
    cap_one = jnp.where(kj1 <= qi1, jnp.inf, -jnp.inf)

    def block(q_idx, k_idx, cap, src, dst):
        width = cap.shape[1]
        q = jnp.concatenate([q_ref[r, q_idx, :] for r in range(Q_PER_KV)], axis=0).astype(BF16)
        k = k_ref[k_idx, :].astype(BF16)
        v = v_ref[k_idx, :].astype(BF16)
        v_ones = jnp.concatenate([v, jnp.ones((width, HEAD_DIM), BF16)], axis=1)
        t = lax.dot_general(q, k, (((1,), (1,)), ((), ())), preferred_element_type=F32)
        t = jnp.minimum(t * logit_scale, cap)
        ps, m_news, a_olds = [], [], []
        for r in range(Q_PER_KV):
            t_r = t[r * SPAN:(r + 1) * SPAN]
            m_blk = jnp.broadcast_to(jnp.max(t_r, axis=-1, keepdims=True), (SPAN, HEAD_DIM))
            if src is None:
                m_new = m_blk
            else:
                m_old = src[1][r, src[3], :]
                m_new = jnp.maximum(m_old, m_blk)
                a_olds.append(jnp.exp2(m_old - m_new))
            m_news.append(m_new)
            p_r = [jnp.exp2(t_r[:, c * HEAD_DIM:(c + 1) * HEAD_DIM] - m_new)
                   for c in range(width // HEAD_DIM)]
            ps.append(jnp.concatenate(p_r, axis=1).astype(BF16))
        o = jnp.dot(jnp.concatenate(ps, axis=0), v_ones, preferred_element_type=F32)
        for r in range(Q_PER_KV):
            acc = o[r * SPAN:(r + 1) * SPAN, :HEAD_DIM]
            den = o[r * SPAN:(r + 1) * SPAN, HEAD_DIM:]
            if src is not None:
                acc = src[0][r, src[3], :] * a_olds[r] + acc
                den = src[2][r, src[3], :] * a_olds[r] + den
            if dst is None:
                o_ref[q_idx, r * HEAD_DIM:(r + 1) * HEAD_DIM] = acc / den
            else:
                dst[0][r, dst[3], :] = acc
                dst[1][r, dst[3], :] = m_news[r]
                dst[2][r, dst[3], :] = den

    near, mid, far = DILATIONS
    quarter = seq // mid
    nb_mid = quarter // SPAN
    class_state = (tacc_ref, tm_ref, tl_ref)
    seq_state = (acc_ref, m_ref, l_ref)

    def far_block(e, carry):
        idx = pl.ds(e, SPAN, stride=far)
        dst_idx = pl.ds((e % mid) * quarter + e // mid, SPAN, stride=mid)
        block(idx, idx, cap_one, None, class_state + (dst_idx,))
        return carry

    lax.fori_loop(0, far, far_block, 0, unroll=ATTN_UNROLL)

    def mid_first(e4, carry):
        idx = pl.ds(e4, SPAN, stride=mid)
        src_idx = pl.ds(pl.multiple_of(e4 * quarter, SPAN), SPAN)
        block(idx, idx, cap_one, class_state + (src_idx,), seq_state + (idx,))
        return carry

    lax.fori_loop(0, mid, mid_first, 0, unroll=ATTN_UNROLL)

    def mid_later(t, carry):
        e4 = t // (nb_mid - 1)
        nblk = t % (nb_mid - 1) + 1
        start = e4 + nblk * (mid * SPAN)
        q_idx = pl.ds(start, SPAN, stride=mid)
        k_idx = pl.ds(start - mid * SPAN, 2 * SPAN, stride=mid)
        src_idx = pl.ds(pl.multiple_of(e4 * quarter + nblk * SPAN, SPAN), SPAN)
        block(q_idx, k_idx, cap_two, class_state + (src_idx,), seq_state + (q_idx,))
        return carry

    lax.fori_loop(0, mid * (nb_mid - 1), mid_later, 0, unroll=ATTN_UNROLL)

    first = pl.ds(0, SPAN)
    block(first, first, cap_one, seq_state + (first,), None)

    def near_later(nblk, carry):
        q_idx = pl.ds(pl.multiple_of(nblk * SPAN, SPAN), SPAN)
        k_idx = pl.ds(pl.multiple_of((nblk - 1) * SPAN, SPAN), 2 * SPAN)
        block(q_idx, k_idx, cap_two, seq_state + (q_idx,), None)
        return carry

    lax.fori_loop(1, seq // (near * SPAN), near_later, 0, unroll=ATTN_UNROLL)


def _attention(q, k, v, batch, seq):
    n = q.shape[1]
    near, mid, far = DILATIONS
    assert near == 1 and far == mid * mid and seq % (far * SPAN) == 0
    slab = Q_PER_KV * seq * HEAD_DIM * 4
    vmem = 2 * slab + 2 * 2 * seq * HEAD_DIM * 4 + 2 * slab + 6 * slab
    return pl.pallas_call(
        _attn_kernel,
        grid=(batch, N_KV_HEADS),
        in_specs=[
            pl.BlockSpec((Q_PER_KV, seq, HEAD_DIM), lambda b, g: (g, b, 0)),
            pl.BlockSpec((None, seq, HEAD_DIM), lambda b, g: (g, b, 0)),
            pl.BlockSpec((None, seq, HEAD_DIM), lambda b, g: (g, b, 0)),
        ],
        out_specs=pl.BlockSpec((seq, Q_PER_KV * HEAD_DIM), lambda b, g: (b, g)),
        out_shape=jax.ShapeDtypeStruct((n, Q_DIM), F32),
        scratch_shapes=[pltpu.VMEM((Q_PER_KV, seq, HEAD_DIM), F32)] * 6,
        compiler_params=pltpu.CompilerParams(
            dimension_semantics=("parallel", "parallel"),
            vmem_limit_bytes=min(vmem + (8 << 20), V7X_VMEM_BYTES - (4 << 20))),
        name="attn",
    )(q, k, v)


def _out_proj_kernel(a_ref, cn_ref, x_ref, ga_ref, w_ref, gpost_ref, o_ref):
    an = _rms(a_ref[...], ga_ref[...]).astype(BF16)
    mixed = jnp.dot(an, w_ref[0:Q_DIM, :], preferred_element_type=F32)
    mixed += jnp.dot(cn_ref[...], w_ref[Q_DIM:, :], preferred_element_type=F32)
    o_ref[...] = x_ref[...] + _rms(mixed, gpost_ref[...])


def _out_proj(a, cn, x, ga, w_out, gpost, layer, *, tm=512):
    n, d = x.shape
    assert n % tm == 0
    vmem = (2 * tm * Q_DIM * 4 + 2 * tm * CONV_DIM * 2 + 4 * tm * d * 4
            + 2 * (Q_DIM + CONV_DIM) * d * 2 + 2 * tm * d * 4)
    return pl.pallas_call(
        _out_proj_kernel,
        grid=(n // tm,),
        in_specs=[
            pl.BlockSpec((tm, Q_DIM), lambda i: (i, 0)),
            pl.BlockSpec((tm, CONV_DIM), lambda i: (i, 0)),
            pl.BlockSpec((tm, d), lambda i: (i, 0)),
            pl.BlockSpec((None, 1, Q_DIM), lambda i: (layer, 0, 0)),
            pl.BlockSpec((None, Q_DIM + CONV_DIM, d), lambda i: (layer, 0, 0)),
            pl.BlockSpec((None, 1, d), lambda i: (layer, 0, 0)),
        ],
        out_specs=pl.BlockSpec((tm, d), lambda i: (i, 0)),
        out_shape=jax.ShapeDtypeStruct((n, d), F32),
        compiler_params=pltpu.CompilerParams(
            dimension_semantics=("parallel",),
            vmem_limit_bytes=min(vmem + (4 << 20), V7X_VMEM_BYTES - (4 << 20))),
        name="out_proj",
    )(a, cn, x, ga, w_out, gpost)


def kernel(x, ffn1_norm_pre, ffn1_w_gate_up, ffn1_w_down, ffn1_norm_post, mix_norm_pre, w_in, conv_w, attn_out_norm, conv_out_norm, w_out, mix_norm_post, ffn2_norm_pre, ffn2_w_gate_up, ffn2_w_down, ffn2_norm_post):
    batch, seq, d = x.shape
    depth = w_in.shape[0]
    xf = x.reshape(batch * seq, d)

    def row(gain):
        return gain.reshape(gain.shape[0], 1, gain.shape[1])

    nj = ffn1_w_down.shape[1] // FFN_HIDDEN_TILE
    w1 = (ffn1_w_gate_up[0].astype(BF16), ffn1_w_down[0].astype(BF16))
    for l in range(depth):
        casts = [_cast_rows(ffn2_w_gate_up, l, GATE_UP_CAST_ROWS, nj),
                 _cast_rows(ffn2_w_down, l, CAST_ROWS, nj)]
        if l == 0:
            casts += [_cast_rows(w_in, None, CAST_ROWS, nj), _cast_rows(w_out, None, CAST_ROWS, nj)]
        xf, done = _ffn(xf, row(ffn1_norm_pre), w1[0], w1[1], row(ffn1_norm_post), l, casts)
        w2 = done[:2]
        if l == 0:
            w_in_b = done[2].reshape(w_in.shape)
            w_out_b = done[3].reshape(w_out.shape)
        q, k, v, cn = _in_proj(xf, row(mix_norm_pre), w_in_b, conv_w, row(conv_out_norm), l, seq)
        a = _attention(q, k, v, batch, seq)
        xf = _out_proj(a, cn, xf, row(attn_out_norm), w_out_b, row(mix_norm_post), l)
        casts = []
        if l + 1 < depth:
            casts = [_cast_rows(ffn1_w_gate_up, l + 1, GATE_UP_CAST_ROWS, nj),
                     _cast_rows(ffn1_w_down, l + 1, CAST_ROWS, nj)]
        xf, w1 = _ffn(xf, row(ffn2_norm_pre), w2[0], w2[1], row(ffn2_norm_post), l, casts)
    return xf.reshape(batch, seq, d)
```

```python
import functools
from typing import Callable, NamedTuple

import jax
import jax.numpy as jnp
from jax import lax
from jax.experimental import pallas as pl
from jax.experimental.pallas import tpu as pltpu

D_MODEL = 2048
HEAD_DIM = 128
ATTN_WIDTH = D_MODEL // 2
CONV_DIM = D_MODEL - ATTN_WIDTH
N_Q_HEADS = ATTN_WIDTH // HEAD_DIM
N_KV_HEADS = max(1, N_Q_HEADS // 4)
Q_PER_KV = N_Q_HEADS // N_KV_HEADS
Q_DIM = N_Q_HEADS * HEAD_DIM
KV_DIM = N_KV_HEADS * HEAD_DIM
CONV_WIDTH = 3
IN_PROJ_DIM = Q_DIM + 2 * KV_DIM + 3 * CONV_DIM
SPAN = 128
DILATIONS = (1, 4, 16)
FFN_RESIDUAL_WEIGHT = 0.5
NORM_EPS = 1e-6
LOG2_E = 1.4426950408889634

V7X_VMEM_BYTES = 64 * 1024 * 1024
SUBLANES = 8
FFN_TOKEN_TILE = 1024
FFN_HIDDEN_TILE = 512
FFN_ROW_SUBTILE = 512
PROJ_ROW_SUBTILE = 256
CAST_ROWS = 32
GATE_UP_CAST_ROWS = 16
NORM_ROW_CHUNK = 128
ATTN_UNROLL = 16

BF16 = jnp.bfloat16
F32 = jnp.float32

_OFF_K = Q_DIM
_OFF_V = Q_DIM + KV_DIM
_OFF_H = Q_DIM + 2 * KV_DIM
_OFF_B = _OFF_H + CONV_DIM
_OFF_C = _OFF_B + CONV_DIM


def _rms(x, gain):
    y = x * lax.rsqrt(jnp.mean(x * x, axis=-1, keepdims=True) + NORM_EPS)
    return y * gain


def _ffn_kernel(*refs, n_casts, fuse_post):
    x_ref, gpre_ref, wg_ref, wu_ref, wd_ref, gpost_ref = refs[:6]
    cast_src = refs[6:6 + n_casts]
    o_ref = refs[6 + n_casts]
    cast_dst = refs[7 + n_casts:7 + 2 * n_casts]
    h_ref = refs[7 + 2 * n_casts]
    j = pl.program_id(1)
    last = pl.num_programs(1) - 1
    n_rows = h_ref.shape[0]

    def step(first_step, last_step):
        for src, dst in zip(cast_src, cast_dst):
            dst[...] = src[...].astype(BF16)
        if last_step:
            half_gain = FFN_RESIDUAL_WEIGHT * gpost_ref[...]
        for r0 in range(0, n_rows, FFN_ROW_SUBTILE):
            if first_step:
                for c0 in range(r0, r0 + FFN_ROW_SUBTILE, NORM_ROW_CHUNK):
                    rows = slice(c0, c0 + NORM_ROW_CHUNK)
                    h_ref[rows, :] = _rms(x_ref[rows, :], gpre_ref[...]).astype(BF16)
            rows = slice(r0, r0 + FFN_ROW_SUBTILE)
            h = h_ref[rows, :]
            g = jnp.dot(h, wg_ref[...], preferred_element_type=F32)
            u = jnp.dot(h, wu_ref[...], preferred_element_type=F32)
            a = (g * jax.nn.sigmoid(g) * u).astype(BF16)
            part = jnp.dot(a, wd_ref[...], preferred_element_type=F32)
            if first_step:
                o_ref[rows, :] = part
            else:
                o_ref[rows, :] += part
            if last_step:
                for c0 in range(r0, r0 + FFN_ROW_SUBTILE, NORM_ROW_CHUNK):
                    rows = slice(c0, c0 + NORM_ROW_CHUNK)
                    o_ref[rows, :] = x_ref[rows, :] + _rms(o_ref[rows, :], half_gain)

    pl.when(j == 0)(lambda: step(True, False))
    if fuse_post:
        pl.when((j > 0) & (j < last))(lambda: step(False, False))
        pl.when(j == last)(lambda: step(False, True))
    else:
        pl.when(j > 0)(lambda: step(False, False))


class _Cast(NamedTuple):
    src: jax.Array
    src_block: tuple
    src_map: Callable
    out_shape: tuple
    out_block: tuple
    out_map: Callable


def _cast_rows(w, layer, block_rows, nj):
    depth, rows, cols = w.shape
    if layer is None:
        w, rows = w.reshape(depth * rows, cols), depth * rows
    assert rows % block_rows == 0
    last_block = rows // block_rows - 1

    def out_map(i, j):
        return (jnp.minimum(i * nj + j, last_block), 0)

    if layer is None:
        return _Cast(w, (block_rows, cols), out_map, (rows, cols), (block_rows, cols), out_map)
    return _Cast(w, (None, block_rows, cols), lambda i, j: (layer,) + out_map(i, j),
                 (rows, cols), (block_rows, cols), out_map)


def _ffn(x, gpre, w_gu, w_down, gpost, layer, casts=(), fuse_post=True):
    n, d = x.shape
    d_ff = w_down.shape[0]
    tm, tf = FFN_TOKEN_TILE, FFN_HIDDEN_TILE
    nj = d_ff // tf
    assert n % tm == 0 and d_ff % tf == 0 and tm % FFN_ROW_SUBTILE == 0
    assert nj >= 2, "the first and the last grid step must be different steps"
    assert all(c.out_shape[0] // c.out_block[0] <= (n // tm) * nj for c in casts)
    cast_bytes = sum(2 * (4 + 2) * c.out_block[0] * c.out_block[1] for c in casts)
    vmem = (2 * 2 * tm * d * 4
            + tm * d * 2
            + 2 * 3 * d * tf * 2
            + 6 * FFN_ROW_SUBTILE * tf * 4
            + cast_bytes)
    outs = pl.pallas_call(
        functools.partial(_ffn_kernel, n_casts=len(casts), fuse_post=fuse_post),
        grid=(n // tm, nj),
        in_specs=[
            pl.BlockSpec((tm, d), lambda i, j: (i, 0)),
            pl.BlockSpec((None, 1, d), lambda i, j: (layer, 0, 0)),
            pl.BlockSpec((d, tf), lambda i, j: (0, j)),
            pl.BlockSpec((d, tf), lambda i, j: (0, j + nj)),
            pl.BlockSpec((tf, d), lambda i, j: (j, 0)),
            pl.BlockSpec((None, 1, d), lambda i, j: (layer, 0, 0)),
        ] + [pl.BlockSpec(c.src_block, c.src_map) for c in casts],
        out_specs=[pl.BlockSpec((tm, d), lambda i, j: (i, 0))]
        + [pl.BlockSpec(c.out_block, c.out_map) for c in casts],
        out_shape=[jax.ShapeDtypeStruct((n, d), F32)]
        + [jax.ShapeDtypeStruct(c.out_shape, BF16) for c in casts],
        scratch_shapes=[pltpu.VMEM((tm, d), BF16)],
        compiler_params=pltpu.CompilerParams(
            dimension_semantics=("arbitrary", "arbitrary"),
            vmem_limit_bytes=min(vmem + (4 << 20), V7X_VMEM_BYTES - (4 << 20))),
        name="ffn",
    )(x, gpre, w_gu, w_gu, w_down, gpost, *[c.src for c in casts])
    return outs[0], outs[1:]


def _in_proj_kernel(x_ref, ffn_ref, gffn_ref, gpre_ref, w_ref, cw_ref, gc_ref,
                    x1_ref, q_ref, k_ref, v_ref, cn_ref, u_ref, *, tiles_per_seq):
    i = pl.program_id(0)
    tm = x_ref.shape[0]
    half_gain = FFN_RESIDUAL_WEIGHT * gffn_ref[...]

    @pl.when(i % tiles_per_seq == 0)
    def _():
        u_ref[0:SUBLANES, :] = jnp.zeros((SUBLANES, CONV_DIM), F32)

    @pl.when(i % tiles_per_seq != 0)
    def _():
        u_ref[0:SUBLANES, :] = u_ref[tm:tm + SUBLANES, :]

    for r0 in range(0, tm, PROJ_ROW_SUBTILE):
        rows = slice(r0, r0 + PROJ_ROW_SUBTILE)
        x1 = x_ref[rows, :] + _rms(ffn_ref[rows, :], half_gain)
        x1_ref[rows, :] = x1
        h = _rms(x1, gpre_ref[...]).astype(BF16)

        def proj(off, width):
            return jnp.dot(h, w_ref[:, off:off + width], preferred_element_type=F32)

        base = SUBLANES + r0
        u_ref[base:base + PROJ_ROW_SUBTILE, :] = proj(_OFF_C, CONV_DIM) * proj(_OFF_H, CONV_DIM)
        y = (cw_ref[2:3, :] * u_ref[base:base + PROJ_ROW_SUBTILE, :]
             + cw_ref[1:2, :] * u_ref[base - 1:base - 1 + PROJ_ROW_SUBTILE, :]
             + cw_ref[0:1, :] * u_ref[base - 2:base - 2 + PROJ_ROW_SUBTILE, :])
        c = proj(_OFF_B, CONV_DIM) * y
        cn_ref[rows, :] = _rms(c, gc_ref[...]).astype(BF16)

        zq = proj(0, Q_DIM)
        for hd in range(N_Q_HEADS):
            q_ref[hd, rows, :] = zq[:, hd * HEAD_DIM:(hd + 1) * HEAD_DIM]
        zk = proj(_OFF_K, KV_DIM)
        zv = proj(_OFF_V, KV_DIM)
        for hd in range(N_KV_HEADS):
            k_ref[hd, rows, :] = zk[:, hd * HEAD_DIM:(hd + 1) * HEAD_DIM]
            v_ref[hd, rows, :] = zv[:, hd * HEAD_DIM:(hd + 1) * HEAD_DIM]


def _in_proj(x, ffn, gffn, gpre, w_in, conv_w, gconv, layer, seq, *, tm=512):
    n, d = x.shape
    assert n % tm == 0 and seq % tm == 0 and tm % PROJ_ROW_SUBTILE == 0
    vmem = (3 * 2 * tm * d * 4 + 2 * d * IN_PROJ_DIM * 2
            + 2 * tm * (Q_DIM + 2 * KV_DIM) * 4 + 2 * tm * CONV_DIM * 2
            + (tm + SUBLANES) * CONV_DIM * 4 + 4 * tm * CONV_DIM * 4)
    row_tile = pl.BlockSpec((tm, d), lambda i: (i, 0))
    layer_row = pl.BlockSpec((None, 1, d), lambda i: (layer, 0, 0))
    return pl.pallas_call(
        functools.partial(_in_proj_kernel, tiles_per_seq=seq // tm),
        grid=(n // tm,),
        in_specs=[
            row_tile,
            row_tile,
            layer_row,
            layer_row,
            pl.BlockSpec((None, d, IN_PROJ_DIM), lambda i: (layer, 0, 0)),
            pl.BlockSpec((None, CONV_WIDTH, CONV_DIM), lambda i: (layer, 0, 0)),
            pl.BlockSpec((None, 1, CONV_DIM), lambda i: (layer, 0, 0)),
        ],
        out_specs=[
            row_tile,
            pl.BlockSpec((N_Q_HEADS, tm, HEAD_DIM), lambda i: (0, i, 0)),
            pl.BlockSpec((N_KV_HEADS, tm, HEAD_DIM), lambda i: (0, i, 0)),
            pl.BlockSpec((N_KV_HEADS, tm, HEAD_DIM), lambda i: (0, i, 0)),
            pl.BlockSpec((tm, CONV_DIM), lambda i: (i, 0)),
        ],
        out_shape=[
            jax.ShapeDtypeStruct((n, d), F32),
            jax.ShapeDtypeStruct((N_Q_HEADS, n, HEAD_DIM), F32),
            jax.ShapeDtypeStruct((N_KV_HEADS, n, HEAD_DIM), F32),
            jax.ShapeDtypeStruct((N_KV_HEADS, n, HEAD_DIM), F32),
            jax.ShapeDtypeStruct((n, CONV_DIM), BF16),
        ],
        scratch_shapes=[pltpu.VMEM((tm + SUBLANES, CONV_DIM), F32)],
        compiler_params=pltpu.CompilerParams(
            dimension_semantics=("arbitrary",),
            vmem_limit_bytes=min(vmem + (4 << 20), V7X_VMEM_BYTES - (4 << 20))),
        name="in_proj",
    )(x, ffn, gffn, gpre, w_in, conv_w, gconv)


def _attn_kernel(q_ref, k_ref, v_ref, o_ref, acc_ref, m_ref, l_ref, tacc_ref, tm_ref, tl_ref):
    seq = k_ref.shape[0]
    logit_scale = (HEAD_DIM ** -0.5) * LOG2_E
    rows = Q_PER_KV * SPAN
    qi = lax.broadcasted_iota(jnp.int32, (rows, 2 * SPAN), 0) % SPAN
    kj = lax.broadcasted_iota(jnp.int32, (rows, 2 * SPAN), 1)
    cap_two = jnp.where((kj >= qi) & (kj <= qi + SPAN), jnp.inf, -jnp.inf)
    qi1 = lax.broadcasted_iota(jnp.int32, (rows, SPAN), 0) % SPAN
    kj1 = lax.broadcasted_iota(jnp.int32, (rows, SPAN), 1)
    cap_one = jnp.where(kj1 <= qi1, jnp.inf, -jnp.inf)

    def block(q_idx, k_idx, cap, src, dst):
        width = cap.shape[1]
        q = jnp.concatenate([q_ref[r, q_idx, :] for r in range(Q_PER_KV)], axis=0).astype(BF16)
        k = k_ref[k_idx, :].astype(BF16)
        v = v_ref[k_idx, :].astype(BF16)
        v_ones = jnp.concatenate([v, jnp.ones((width, HEAD_DIM), BF16)], axis=1)
        t = lax.dot_general(q, k, (((1,), (1,)), ((), ())), preferred_element_type=F32)
        t = jnp.minimum(t * logit_scale, cap)
        ps, m_news, a_olds = [], [], []
        for r in range(Q_PER_KV):
            t_r = t[r * SPAN:(r + 1) * SPAN]
            m_blk = jnp.broadcast_to(jnp.max(t_r, axis=-1, keepdims=True), (SPAN, HEAD_DIM))
            if src is None:
                m_new = m_blk
            else:
                m_old = src[1][r, src[3], :]
                m_new = jnp.maximum(m_old, m_blk)
                a_olds.append(jnp.exp2(m_old - m_new))
            m_news.append(m_new)
            p_r = [jnp.exp2(t_r[:, c * HEAD_DIM:(c + 1) * HEAD_DIM] - m_new)
                   for c in range(width // HEAD_DIM)]
            ps.append(jnp.concatenate(p_r, axis=1).astype(BF16))
        o = jnp.dot(jnp.concatenate(ps, axis=0), v_ones, preferred_element_type=F32)
        for r in range(Q_PER_KV):
            acc = o[r * SPAN:(r + 1) * SPAN, :HEAD_DIM]
            den = o[r * SPAN:(r + 1) * SPAN, HEAD_DIM:]
            if src is not None:
                acc = src[0][r, src[3], :] * a_olds[r] + acc
                den = src[2][r, src[3], :] * a_olds[r] + den
            if dst is None:
                o_ref[q_idx, r * HEAD_DIM:(r + 1) * HEAD_DIM] = acc / den
            else:
                dst[0][r, dst[3], :] = acc
                dst[1][r, dst[3], :] = m_news[r]
                dst[2][r, dst[3], :] = den

    near, mid, far = DILATIONS
    quarter = seq // mid
    nb_mid = quarter // SPAN
    class_state = (tacc_ref, tm_ref, tl_ref)
    seq_state = (acc_ref, m_ref, l_ref)

    def far_block(e, carry):
        idx = pl.ds(e, SPAN, stride=far)
        dst_idx = pl.ds((e % mid) * quarter + e // mid, SPAN, stride=mid)
        block(idx, idx, cap_one, None, class_state + (dst_idx,))
        return carry

    lax.fori_loop(0, far, far_block, 0, unroll=ATTN_UNROLL)

    def mid_first(e4, carry):
        idx = pl.ds(e4, SPAN, stride=mid)
        src_idx = pl.ds(pl.multiple_of(e4 * quarter, SPAN), SPAN)
        block(idx, idx, cap_one, class_state + (src_idx,), seq_state + (idx,))
        return carry

    lax.fori_loop(0, mid, mid_first, 0, unroll=ATTN_UNROLL)

    def mid_later(t, carry):
        e4 = t // (nb_mid - 1)
        nblk = t % (nb_mid - 1) + 1
        start = e4 + nblk * (mid * SPAN)
        q_idx = pl.ds(start, SPAN, stride=mid)
        k_idx = pl.ds(start - mid * SPAN, 2 * SPAN, stride=mid)
        src_idx = pl.ds(pl.multiple_of(e4 * quarter + nblk * SPAN, SPAN), SPAN)
        block(q_idx, k_idx, cap_two, class_state + (src_idx,), seq_state + (q_idx,))
        return carry

    lax.fori_loop(0, mid * (nb_mid - 1), mid_later, 0, unroll=ATTN_UNROLL)

    first = pl.ds(0, SPAN)
    block(first, first, cap_one, seq_state + (first,), None)

    def near_later(nblk, carry):
        q_idx = pl.ds(pl.multiple_of(nblk * SPAN, SPAN), SPAN)
        k_idx = pl.ds(pl.multiple_of((nblk - 1) * SPAN, SPAN), 2 * SPAN)
        block(q_idx, k_idx, cap_two, seq_state + (q_idx,), None)
        return carry

    lax.fori_loop(1, seq // (near * SPAN), near_later, 0, unroll=ATTN_UNROLL)


def _attention(q, k, v, batch, seq):
    n = q.shape[1]
    near, mid, far = DILATIONS
    assert near == 1 and far == mid * mid and seq % (far * SPAN) == 0
    slab = Q_PER_KV * seq * HEAD_DIM * 4
    vmem = 2 * slab + 2 * 2 * seq * HEAD_DIM * 4 + 2 * slab + 6 * slab
    return pl.pallas_call(
        _attn_kernel,
        grid=(batch, N_KV_HEADS),
        in_specs=[
            pl.BlockSpec((Q_PER_KV, seq, HEAD_DIM), lambda b, g: (g, b, 0)),
            pl.BlockSpec((None, seq, HEAD_DIM), lambda b, g: (g, b, 0)),
            pl.BlockSpec((None, seq, HEAD_DIM), lambda b, g: (g, b, 0)),
        ],
        out_specs=pl.BlockSpec((seq, Q_PER_KV * HEAD_DIM), lambda b, g: (b, g)),
        out_shape=jax.ShapeDtypeStruct((n, Q_DIM), F32),
        scratch_shapes=[pltpu.VMEM((Q_PER_KV, seq, HEAD_DIM), F32)] * 6,
        compiler_params=pltpu.CompilerParams(
            dimension_semantics=("parallel", "parallel"),
            vmem_limit_bytes=min(vmem + (8 << 20), V7X_VMEM_BYTES - (4 << 20))),
        name="attn",
    )(q, k, v)


def _out_proj_kernel(a_ref, cn_ref, x_ref, ga_ref, w_ref, gpost_ref, o_ref):
    an = _rms(a_ref[...], ga_ref[...]).astype(BF16)
    mixed = jnp.dot(an, w_ref[0:Q_DIM, :], preferred_element_type=F32)
    mixed += jnp.dot(cn_ref[...], w_ref[Q_DIM:, :], preferred_element_type=F32)
    o_ref[...] = x_ref[...] + _rms(mixed, gpost_ref[...])


def _out_proj(a, cn, x, ga, w_out, gpost, layer, *, tm=512):
    n, d = x.shape
    assert n % tm == 0
    vmem = (2 * tm * Q_DIM * 4 + 2 * tm * CONV_DIM * 2 + 4 * tm * d * 4
            + 2 * (Q_DIM + CONV_DIM) * d * 2 + 2 * tm * d * 4)
    return pl.pallas_call(
        _out_proj_kernel,
        grid=(n // tm,),
        in_specs=[
            pl.BlockSpec((tm, Q_DIM), lambda i: (i, 0)),
            pl.BlockSpec((tm, CONV_DIM), lambda i: (i, 0)),
            pl.BlockSpec((tm, d), lambda i: (i, 0)),
            pl.BlockSpec((None, 1, Q_DIM), lambda i: (layer, 0, 0)),
            pl.BlockSpec((None, Q_DIM + CONV_DIM, d), lambda i: (layer, 0, 0)),
            pl.BlockSpec((None, 1, d), lambda i: (layer, 0, 0)),
        ],
        out_specs=pl.BlockSpec((tm, d), lambda i: (i, 0)),
        out_shape=jax.ShapeDtypeStruct((n, d), F32),
        compiler_params=pltpu.CompilerParams(
            dimension_semantics=("parallel",),
            vmem_limit_bytes=min(vmem + (4 << 20), V7X_VMEM_BYTES - (4 << 20))),
        name="out_proj",
    )(a, cn, x, ga, w_out, gpost)


def kernel(x, ffn1_norm_pre, ffn1_w_gate_up, ffn1_w_down, ffn1_norm_post, mix_norm_pre, w_in, conv_w, attn_out_norm, conv_out_norm, w_out, mix_norm_post, ffn2_norm_pre, ffn2_w_gate_up, ffn2_w_down, ffn2_norm_post):
    batch, seq, d = x.shape
    depth = w_in.shape[0]
    xf = x.reshape(batch * seq, d)

    def row(gain):
        return gain.reshape(gain.shape[0], 1, gain.shape[1])

    nj = ffn1_w_down.shape[1] // FFN_HIDDEN_TILE
    w1 = (ffn1_w_gate_up[0].astype(BF16), ffn1_w_down[0].astype(BF16))
    for l in range(depth):
        casts = [_cast_rows(ffn2_w_gate_up, l, GATE_UP_CAST_ROWS, nj),
                 _cast_rows(ffn2_w_down, l, CAST_ROWS, nj)]
        if l == 0:
            casts += [_cast_rows(w_in, None, CAST_ROWS, nj), _cast_rows(w_out, None, CAST_ROWS, nj)]
        ffn1, done = _ffn(xf, row(ffn1_norm_pre), w1[0], w1[1], row(ffn1_norm_post), l, casts,
                          fuse_post=False)
        w2 = done[:2]
        if l == 0:
            w_in_b = done[2].reshape(w_in.shape)
            w_out_b = done[3].reshape(w_out.shape)
        xf, q, k, v, cn = _in_proj(xf, ffn1, row(ffn1_norm_post), row(mix_norm_pre), w_in_b, conv_w,
                                   row(conv_out_norm), l, seq)
        a = _attention(q, k, v, batch, seq)
        xf = _out_proj(a, cn, xf, row(attn_out_norm), w_out_b, row(mix_norm_post), l)
        casts = []
        if l + 1 < depth:
            casts = [_cast_rows(ffn1_w_gate_up, l + 1, GATE_UP_CAST_ROWS, nj),
                     _cast_rows(ffn1_w_down, l + 1, CAST_ROWS, nj)]
        xf, w1 = _ffn(xf, row(ffn2_norm_pre), w2[0], w2[1], row(ffn2_norm_post), l, casts)
    return xf.reshape(batch, seq, d)
```
